```python
import jax, jax.numpy as jnp
from jax import lax
import numpy as np

D_MODEL = 2048
BATCH = 2
SEQ = 4096
DEPTH = 2

MEM_LEN = 256
D_MIX = D_MODEL
CONV_CH = D_MIX // 2
CONV_W = 3
GLA_HEADS = 4
GLA_DV_TOTAL = D_MIX - CONV_CH
GLA_DV = GLA_DV_TOTAL // GLA_HEADS
GLA_DK = GLA_DV // 2
GLA_DK_TOTAL = GLA_HEADS * GLA_DK
GLA_GATE_RANK = 16
GLA_TAU = 16.0
GLA_CHUNK = 64
IN_COLS = 3 * CONV_CH + 2 * GLA_DK_TOTAL + 2 * GLA_DV_TOTAL + GLA_GATE_RANK
XA_HEADS = 4
XA_HEAD_DIM = 128
XA_DIM = XA_HEADS * XA_HEAD_DIM
N_EXPERTS = 32
TOP_K = 4
D_FF = D_MODEL
SWIGLU_LIMIT = 7.0
SWIGLU_ALPHA = 1.702
MOE_BLOCK = 128
DN_ALPHA = (2 * DEPTH) ** 0.25
DN_BETA = (8 * DEPTH) ** -0.25
LN_EPS = 1e-5
NORM_EPS = 1e-6

kernel_name = "hybrid_conv_gla_xattn_moe_deepnorm"


def _layer_norm(x, g, b):
    xf = x.astype(jnp.float32)
    mu = jnp.mean(xf, axis=-1, keepdims=True)
    xc = xf - mu
    var = jnp.mean(xc * xc, axis=-1, keepdims=True)
    y = xc * lax.rsqrt(var + LN_EPS) * g.astype(jnp.float32) + b.astype(jnp.float32)
    return y.astype(x.dtype)


def _gla(q, k, v, log_a):
    bsz, seq = q.shape[0], q.shape[1]
    n_chunks = seq // GLA_CHUNK

    def to_chunks(t):
        return t.reshape(bsz, n_chunks, GLA_CHUNK, GLA_HEADS, t.shape[-1]).transpose(1, 0, 3, 2, 4)

    causal = jnp.tril(jnp.ones((GLA_CHUNK, GLA_CHUNK), dtype=bool))[:, :, None]

    def step(state, inp):
        qc, kc, vc, gc = (t.astype(jnp.float32) for t in inp)
        b = jnp.cumsum(gc, axis=2)
        o_inter = jnp.einsum("bhcd,bhde->bhce", qc * jnp.exp(b), state)
        diff = b[:, :, :, None, :] - b[:, :, None, :, :]
        decay = jnp.exp(jnp.where(causal, diff, -jnp.inf))
        scores = jnp.einsum("bhid,bhjd,bhijd->bhij", qc, kc, decay)
        o_intra = jnp.einsum("bhij,bhje->bhie", scores, vc)
        b_last = b[:, :, -1:, :]
        k_dec = kc * jnp.exp(b_last - b)
        state = jnp.exp(b_last[:, :, 0, :])[..., None] * state + jnp.einsum("bhcd,bhce->bhde", k_dec, vc)
        return state, o_inter + o_intra

    state0 = jnp.zeros((bsz, GLA_HEADS, GLA_DK, GLA_DV), jnp.float32)
    _, o = lax.scan(step, state0, (to_chunks(q), to_chunks(k), to_chunks(v), to_chunks(log_a)))
    return o.transpose(1, 0, 3, 2, 4).reshape(bsz, seq, GLA_HEADS, GLA_DV)


def _hybrid_mixer(x, w_in, conv_w, w_gate2, b_gate2, gla_norm_g, w_out):
    bsz, seq, _ = x.shape
    proj = x @ w_in
    sizes = (CONV_CH, CONV_CH, CONV_CH, GLA_DK_TOTAL, GLA_DK_TOTAL, GLA_DV_TOTAL, GLA_DV_TOTAL, GLA_GATE_RANK)
    cuts = [int(c) for c in np.cumsum(sizes)[:-1]]
    gate_c, gate_b, h, q, k, v, r, g_low = jnp.split(proj, cuts, axis=-1)
    u = gate_c * h
    conv = lax.conv_general_dilated(
        u, conv_w.reshape(CONV_W, 1, CONV_CH), window_strides=(1,),
        padding=[(CONV_W - 1, 0)], dimension_numbers=("NWC", "WIO", "NWC"),
        feature_group_count=CONV_CH)
    y_conv = gate_b * conv
    log_a = jax.nn.log_sigmoid((g_low @ w_gate2 + b_gate2).astype(jnp.float32)) / GLA_TAU
    o = _gla(q.reshape(bsz, seq, GLA_HEADS, GLA_DK) * GLA_DK ** -0.5,
             k.reshape(bsz, seq, GLA_HEADS, GLA_DK),
             v.reshape(bsz, seq, GLA_HEADS, GLA_DV),
             log_a.reshape(bsz, seq, GLA_HEADS, GLA_DK))
    o = o * lax.rsqrt(jnp.mean(o * o, axis=-1, keepdims=True) + NORM_EPS) * gla_norm_g.astype(jnp.float32)
    y_gla = o.reshape(bsz, seq, GLA_DV_TOTAL).astype(x.dtype) * jax.nn.silu(r)
    return jnp.concatenate([y_conv, y_gla], axis=-1) @ w_out


def _cross_attn(x, mem, w_xq, w_xkv, w_xo):
    bsz, seq, _ = x.shape
    q = (x @ w_xq).reshape(bsz, seq, XA_HEADS, XA_HEAD_DIM)
    kv = (mem @ w_xkv).reshape(bsz, mem.shape[1], 2, XA_HEADS, XA_HEAD_DIM)
    k, v = kv[:, :, 0], kv[:, :, 1]
    s = jnp.einsum("bshd,bmhd->bhsm", q, k).astype(jnp.float32) * XA_HEAD_DIM ** -0.5
    p = jax.nn.softmax(s, axis=-1).astype(v.dtype)
    o = jnp.einsum("bhsm,bmhd->bshd", p, v).reshape(bsz, seq, XA_DIM)
    return o @ w_xo


def _moe(x, layer, w_router, b_router, w_gu, b_gu, w_down, b_down):
    bsz, seq, d = x.shape
    n_tok = bsz * seq
    xf = x.reshape(n_tok, d)
    logits = (xf @ w_router[layer] + b_router[layer]).astype(jnp.float32)
    top_val, top_idx = lax.top_k(logits, TOP_K)
    gate = jax.nn.softmax(top_val, axis=-1).astype(x.dtype)
    n_asg = n_tok * TOP_K
    flat_e = top_idx.reshape(n_asg)
    flat_tok = jnp.arange(n_asg, dtype=jnp.int32) // TOP_K
    flat_w = gate.reshape(n_asg)
    order = jnp.argsort(flat_e)
    sorted_e = flat_e[order]
    counts = jnp.zeros((N_EXPERTS,), jnp.int32).at[flat_e].add(1)
    padded = ((counts + MOE_BLOCK - 1) // MOE_BLOCK) * MOE_BLOCK
    starts = jnp.cumsum(counts) - counts
    pad_ends = jnp.cumsum(padded)
    pad_starts = pad_ends - padded
    dest = pad_starts[sorted_e] + (jnp.arange(n_asg, dtype=jnp.int32) - starts[sorted_e])
    n_slots = n_asg + N_EXPERTS * MOE_BLOCK
    n_blocks = n_slots // MOE_BLOCK
    slot_tok = jnp.full((n_slots,), n_tok, jnp.int32).at[dest].set(flat_tok[order])
    slot_w = jnp.zeros((n_slots,), x.dtype).at[dest].set(flat_w[order])
    block_e = jnp.clip(jnp.searchsorted(pad_ends, jnp.arange(n_blocks) * MOE_BLOCK, side="right"), 0, N_EXPERTS - 1)
    x_pad = jnp.concatenate([xf, jnp.zeros((1, d), x.dtype)], axis=0)
    xb = x_pad[slot_tok].reshape(n_blocks, MOE_BLOCK, d)

    def expert_block(args):
        xblk, e = args
        gu = xblk @ w_gu[layer, e] + b_gu[layer, e]
        g_ = jnp.minimum(gu[:, 0::2], SWIGLU_LIMIT)
        up = jnp.clip(gu[:, 1::2], -SWIGLU_LIMIT, SWIGLU_LIMIT)
        act = (up + 1.0) * (g_ * jax.nn.sigmoid(g_ * SWIGLU_ALPHA))
        return act @ w_down[layer, e] + b_down[layer, e]

    yb = lax.map(expert_block, (xb, block_e))
    y = jax.ops.segment_sum(yb.reshape(n_slots, d) * slot_w[:, None], slot_tok, num_segments=n_tok + 1)[:n_tok]
    return y.reshape(bsz, seq, d)


def setup_inputs(seed: int = 0) -> dict:
    key = jax.random.key(seed)
    ks = jax.random.split(key, 24)
    f32 = jnp.float32
    nrm = lambda k, shape, scale: jax.random.normal(k, shape, f32) * scale
    gain = lambda k, shape: 1.0 + 0.02 * jax.random.normal(k, shape, f32)
    L = DEPTH
    return {
        "x": jax.random.normal(ks[0], (BATCH, SEQ, D_MODEL), f32),
        "mem": jax.random.normal(ks[1], (BATCH, MEM_LEN, D_MODEL), f32),
        "w_in": nrm(ks[2], (L, D_MODEL, IN_COLS), D_MODEL ** -0.5),
        "conv_w": nrm(ks[3], (L, CONV_W, CONV_CH), CONV_W ** -0.5),
        "w_gate2": nrm(ks[4], (L, GLA_GATE_RANK, GLA_DK_TOTAL), GLA_GATE_RANK ** -0.5),
        "b_gate2": nrm(ks[5], (L, GLA_DK_TOTAL), 0.02),
        "gla_norm_g": gain(ks[6], (L, GLA_HEADS, GLA_DV)),
        "w_out": nrm(ks[7], (L, D_MIX, D_MODEL), D_MIX ** -0.5 * DN_BETA),
        "ln_mix_g": gain(ks[8], (L, D_MODEL)),
        "ln_mix_b": nrm(ks[9], (L, D_MODEL), 0.02),
        "w_xq": nrm(ks[10], (L, D_MODEL, XA_DIM), D_MODEL ** -0.5),
        "w_xkv": nrm(ks[11], (L, D_MODEL, 2 * XA_DIM), D_MODEL ** -0.5),
        "w_xo": nrm(ks[12], (L, XA_DIM, D_MODEL), XA_DIM ** -0.5 * DN_BETA),
        "ln_xa_g": gain(ks[13], (L, D_MODEL)),
        "ln_xa_b": nrm(ks[14], (L, D_MODEL), 0.02),
        "w_router": nrm(ks[15], (L, D_MODEL, N_EXPERTS), D_MODEL ** -0.5),
        "b_router": nrm(ks[16], (L, N_EXPERTS), 0.01),
        "w_gu": nrm(ks[17], (L, N_EXPERTS, D_MODEL, 2 * D_FF), D_MODEL ** -0.5),
        "b_gu": nrm(ks[18], (L, N_EXPERTS, 2 * D_FF), 0.02),
        "w_down": nrm(ks[19], (L, N_EXPERTS, D_FF, D_MODEL), D_FF ** -0.5 * DN_BETA),
        "b_down": nrm(ks[20], (L, N_EXPERTS, D_MODEL), 0.02),
        "ln_moe_g": gain(ks[21], (L, D_MODEL)),
        "ln_moe_b": nrm(ks[22], (L, D_MODEL), 0.02),
    }


def reference(x, mem, w_in, conv_w, w_gate2, b_gate2, gla_norm_g, w_out, ln_mix_g, ln_mix_b,
              w_xq, w_xkv, w_xo, ln_xa_g, ln_xa_b, w_router, b_router, w_gu, b_gu, w_down, b_down,
              ln_moe_g, ln_moe_b):
    for l in range(DEPTH):
        mix = _hybrid_mixer(x, w_in[l], conv_w[l], w_gate2[l], b_gate2[l], gla_norm_g[l], w_out[l])
        x = _layer_norm(DN_ALPHA * x + mix, ln_mix_g[l], ln_mix_b[l])
        xa = _cross_attn(x, mem, w_xq[l], w_xkv[l], w_xo[l])
        x = _layer_norm(DN_ALPHA * x + xa, ln_xa_g[l], ln_xa_b[l])
        ff = _moe(x, l, w_router, b_router, w_gu, b_gu, w_down, b_down)
        x = _layer_norm(DN_ALPHA * x + ff, ln_moe_g[l], ln_moe_b[l])
    return x
```

```python
import functools

import numpy as np
import jax
import jax.numpy as jnp
from jax import lax
from jax.experimental import pallas as pl
from jax.experimental.pallas import tpu as pltpu

CONV_W = 3
GLA_HEADS = 4
GLA_GATE_RANK = 16
GLA_TAU = 16.0
XA_HEADS = 4
TOP_K = 4
SWIGLU_LIMIT = 7.0
SWIGLU_ALPHA = 1.702
LN_EPS = 1e-5
NORM_EPS = 1e-6

LANE = 128
GLA_CHUNK = 128
GLA_LEVELS = 7
MOE_SUPER = 1024
MOE_SUB = 256
VMEM_LIMIT = 56 * 1024 * 1024

BF16 = jnp.bfloat16
F32 = jnp.float32


def _dot(a, b):
    return jnp.dot(a, b, preferred_element_type=F32)


def _dot_nt(a, b):
    return lax.dot_general(a, b, (((1,), (1,)), ((), ())), preferred_element_type=F32)


def _dot_tn(a, b):
    return lax.dot_general(a, b, (((0,), (0,)), ((), ())), preferred_element_type=F32)


def _params(*sem):
    return pltpu.CompilerParams(dimension_semantics=sem, vmem_limit_bytes=VMEM_LIMIT)


def _layer_norm_rows(y, g, b):
    mu = jnp.mean(y, axis=-1, keepdims=True)
    yc = y - mu
    var = jnp.mean(yc * yc, axis=-1, keepdims=True)
    return yc * lax.rsqrt(var + LN_EPS) * g + b


def _matmul_body(a_ref, w_ref, o_ref, abf_ref):
    @pl.when(pl.program_id(1) == 0)
    def _():
        abf_ref[...] = a_ref[...].astype(BF16)

    o_ref[...] = _dot(abf_ref[...], w_ref[...]).astype(o_ref.dtype)


def _matmul(a, w, tm, tn, name):
    m, k = a.shape
    n = w.shape[1]
    return pl.pallas_call(
        _matmul_body,
        grid=(m // tm, n // tn),
        in_specs=[pl.BlockSpec((tm, k), lambda i, j: (i, 0)),
                  pl.BlockSpec((k, tn), lambda i, j: (0, j))],
        out_specs=pl.BlockSpec((tm, tn), lambda i, j: (i, j)),
        out_shape=jax.ShapeDtypeStruct((m, n), F32),
        scratch_shapes=[pltpu.VMEM((tm, k), BF16)],
        compiler_params=_params("parallel", "arbitrary"),
        name=name,
    )(a, w)


def _gate_body(x_ref, w1_ref, w2_ref, b_ref, o_ref):
    low = _dot(x_ref[...].astype(BF16), w1_ref[...])
    z = _dot(low.astype(BF16), w2_ref[...]) + b_ref[...]
    log_sig = jnp.minimum(z, 0.0) - jnp.log1p(jnp.exp(-jnp.abs(z)))
    o_ref[...] = log_sig / GLA_TAU


def _gla_gate(x, w1p, w2p, b2, tm):
    m, d = x.shape
    n = w2p.shape[1]
    return pl.pallas_call(
        _gate_body,
        grid=(m // tm,),
        in_specs=[pl.BlockSpec((tm, d), lambda i: (i, 0)),
                  pl.BlockSpec(w1p.shape, lambda i: (0, 0)),
                  pl.BlockSpec(w2p.shape, lambda i: (0, 0)),
                  pl.BlockSpec((1, n), lambda i: (0, 0))],
        out_specs=pl.BlockSpec((tm, n), lambda i: (i, 0)),
        out_shape=jax.ShapeDtypeStruct((m, n), F32),
        compiler_params=_params("parallel"),
        name="gla_gate",
    )(x, w1p, w2p, b2)


def _conv_body(gc_ref, gb_ref, h_ref, gch_ref, hh_ref, w_ref, o_ref):
    tt = gc_ref.shape[0]
    u = gc_ref[...] * h_ref[...]
    uh = jnp.where(pl.program_id(1) == 0, 0.0, gch_ref[...] * hh_ref[...])
    row = lax.broadcasted_iota(jnp.int32, u.shape, 0)
    u1 = jnp.where(row == 0, uh[7:8, :], pltpu.roll(u, 1, 0))
    u2 = pltpu.roll(u, 2, 0)
    u2 = jnp.where(row == 0, uh[6:7, :], jnp.where(row == 1, uh[7:8, :], u2))
    w = w_ref[...]
    conv = w[0:1, :] * u2 + w[1:2, :] * u1 + w[2:3, :] * u
    o_ref[...] = (gb_ref[...] * conv).astype(o_ref.dtype)
    del tt


def _short_conv(proj, conv_w, bsz, seq, ch, tt, cw):
    nt, nc = seq // tt, ch // cw

    def main(off):
        return pl.BlockSpec((tt, cw), lambda b, i, c: (b * nt + i, off * nc + c))

    def halo(off):
        return pl.BlockSpec(
            (8, cw), lambda b, i, c: (jnp.maximum((b * seq + i * tt) // 8 - 1, 0), off * nc + c))

    return pl.pallas_call(
        _conv_body,
        grid=(bsz, nt, nc),
        in_specs=[main(0), main(1), main(2), halo(0), halo(2),
                  pl.BlockSpec((CONV_W, cw), lambda b, i, c: (0, c))],
        out_specs=pl.BlockSpec((tt, cw), lambda b, i, c: (b * nt + i, c)),
        out_shape=jax.ShapeDtypeStruct((bsz * seq, ch), BF16),
        compiler_params=_params("parallel", "parallel", "parallel"),
        name="short_conv",
    )(proj, proj, proj, proj, proj, conv_w)


def _gla_tables():
    c = GLA_CHUNK
    i = np.arange(c)[:, None]
    t = np.arange(c)[None, :]
    mats = []
    for lvl in range(GLA_LEVELS):
        m = ((i >> (lvl + 1)) << (lvl + 1)) + (1 << lvl) - 1
        upper = (i > m) & (t > m) & (t <= i)
        lower = (i <= m) & (t > i) & (t <= m)
        mats.append(upper | lower)
    mats.append(t <= i)
    return jnp.asarray(np.concatenate(mats, axis=0).astype(np.float32), dtype=BF16)


def _gla_body(q_ref, k_ref, g_ref, v_ref, r_ref, gn_ref, tab_ref, o_ref, st_ref, *, dk_scale):
    c = GLA_CHUNK
    dk = q_ref.shape[1]

    @pl.when(pl.program_id(2) == 0)
    def _():
        st_ref[...] = jnp.zeros_like(st_ref)

    ii = lax.broadcasted_iota(jnp.int32, (c, c), 0)
    jj = lax.broadcasted_iota(jnp.int32, (c, c), 1)
    xor = ii ^ jj
    lower = ii > jj

    for ch in range(q_ref.shape[0] // c):
        rows = pl.ds(ch * c, c)
        q = q_ref[rows, :] * dk_scale
        k = k_ref[rows, :]
        g = g_ref[rows, :]
        v_bf = v_ref[rows, :].astype(BF16)
        g_hi = g.astype(BF16)
        g_lo = (g - g_hi.astype(F32)).astype(BF16)
        sums = _dot(tab_ref[...], jnp.concatenate([g_hi, g_lo], axis=1))
        sums = sums[:, :dk] + sums[:, dk:]

        scores = jnp.where(ii == jj, _dot_nt(q.astype(BF16), k.astype(BF16)), 0.0)
        for lvl in range(GLA_LEVELS):
            w = jnp.exp(sums[lvl * c:(lvl + 1) * c, :])
            s_l = _dot_nt((q * w).astype(BF16), (k * w).astype(BF16))
            scores = jnp.where(((xor >> lvl) == 1) & lower, s_l, scores)

        b = sums[GLA_LEVELS * c:, :]
        b_last = b[c - 1:c, :]
        st = st_ref[...]
        o = _dot_nt((q * jnp.exp(b)).astype(BF16), st.astype(BF16)) + _dot(scores.astype(BF16), v_bf)
        k_dec = (k * jnp.exp(b_last - b)).astype(BF16)
        st_ref[...] = st * jnp.exp(b_last) + _dot_tn(v_bf, k_dec)

        o = o * lax.rsqrt(jnp.mean(o * o, axis=-1, keepdims=True) + NORM_EPS) * gn_ref[...]
        r = r_ref[rows, :]
        o_ref[rows, :] = (o * (r / (1.0 + jnp.exp(-r)))).astype(o_ref.dtype)


def _gla(proj, log_a, gn, bsz, seq, q_off, k_off, v_off, r_off, dk, dv, tt):
    nt = seq // tt
    tab = _gla_tables()

    def spec(width, off):
        return pl.BlockSpec((tt, width), lambda b, h, i: (b * nt + i, off // width + h))

    return pl.pallas_call(
        functools.partial(_gla_body, dk_scale=dk ** -0.5),
        grid=(bsz, GLA_HEADS, nt),
        in_specs=[spec(dk, q_off), spec(dk, k_off), spec(dk, 0), spec(dv, v_off), spec(dv, r_off),
                  pl.BlockSpec((None, 1, dv), lambda b, h, i: (h, 0, 0)),
                  pl.BlockSpec(tab.shape, lambda b, h, i: (0, 0))],
        out_specs=pl.BlockSpec((tt, dv), lambda b, h, i: (b * nt + i, h)),
        out_shape=jax.ShapeDtypeStruct((bsz * seq, GLA_HEADS * dv), BF16),
        scratch_shapes=[pltpu.VMEM((dv, dk), F32)],
        compiler_params=_params("parallel", "parallel", "arbitrary"),
        name="gla",
    )(proj, proj, log_a, proj, proj, gn.reshape(GLA_HEADS, 1, dv), tab)


def _proj_ln_body(a0_ref, a1_ref, w0_ref, w1_ref, x_ref, g_ref, b_ref, o_ref, *, alpha):
    y = _dot(a0_ref[...], w0_ref[...]) + _dot(a1_ref[...], w1_ref[...])
    o_ref[...] = _layer_norm_rows(alpha * x_ref[...] + y, g_ref[...], b_ref[...])


def _proj_ln(a0, a1, w0, w1, x, g, b, alpha, tm):
    m, d = x.shape
    row = lambda i: (i, 0)
    fixed = lambda i: (0, 0)
    return pl.pallas_call(
        functools.partial(_proj_ln_body, alpha=alpha),
        grid=(m // tm,),
        in_specs=[pl.BlockSpec((tm, a0.shape[1]), row), pl.BlockSpec((tm, a1.shape[1]), row),
                  pl.BlockSpec(w0.shape, fixed), pl.BlockSpec(w1.shape, fixed),
                  pl.BlockSpec((tm, d), row), pl.BlockSpec((1, d), fixed), pl.BlockSpec((1, d), fixed)],
        out_specs=pl.BlockSpec((tm, d), row),
        out_shape=jax.ShapeDtypeStruct((m, d), F32),
        compiler_params=_params("parallel"),
        name="mixer_out_ln",
    )(a0, a1, w0, w1, x, g.reshape(1, d), b.reshape(1, d))


def _xattn_body(x_ref, kv_ref, wq_ref, wo_ref, g_ref, b_ref, o_ref, *, alpha):
    x = x_ref[...]
    xa = wq_ref.shape[1]
    hd = xa // XA_HEADS
    q = _dot(x.astype(BF16), wq_ref[...])
    heads = []
    for h in range(XA_HEADS):
        qh = q[:, h * hd:(h + 1) * hd].astype(BF16)
        kh = kv_ref[:, h * hd:(h + 1) * hd].astype(BF16)
        vh = kv_ref[:, xa + h * hd:xa + (h + 1) * hd].astype(BF16)
        s = _dot_nt(qh, kh) * hd ** -0.5
        p = jnp.exp(s - jnp.max(s, axis=-1, keepdims=True))
        p = p / jnp.sum(p, axis=-1, keepdims=True)
        heads.append(_dot(p.astype(BF16), vh))
    o = jnp.concatenate(heads, axis=1).astype(BF16)
    y = _dot(o, wo_ref[...])
    o_ref[...] = _layer_norm_rows(alpha * x + y, g_ref[...], b_ref[...])


def _cross_attn(x, kv, wq, wo, g, b, alpha, bsz, seq, tm):
    m, d = x.shape
    nt = seq // tm
    mem_len = kv.shape[0] // bsz
    fixed = lambda bb, i: (0, 0)
    return pl.pallas_call(
        functools.partial(_xattn_body, alpha=alpha),
        grid=(bsz, nt),
        in_specs=[pl.BlockSpec((tm, d), lambda bb, i: (bb * nt + i, 0)),
                  pl.BlockSpec((mem_len, kv.shape[1]), lambda bb, i: (bb, 0)),
                  pl.BlockSpec(wq.shape, fixed), pl.BlockSpec(wo.shape, fixed),
                  pl.BlockSpec((1, d), fixed), pl.BlockSpec((1, d), fixed)],
        out_specs=pl.BlockSpec((tm, d), lambda bb, i: (bb * nt + i, 0)),
        out_shape=jax.ShapeDtypeStruct((m, d), F32),
        compiler_params=_params("parallel", "parallel"),
        name="cross_attn_ln",
    )(x, kv, wq, wo, g.reshape(1, d), b.reshape(1, d))


def _router_body(x_ref, wt_ref, b_ref, idx_ref, gate_ref):
    logits = _dot_nt(wt_ref[...], x_ref[...].astype(BF16)) + b_ref[...]
    n_e = logits.shape[0]
    e_iota = lax.broadcasted_iota(jnp.int32, logits.shape, 0)
    vals, idxs = [], []
    for _ in range(TOP_K):
        best = jnp.max(logits, axis=0, keepdims=True)
        sel = jnp.min(jnp.where(logits == best, e_iota, n_e), axis=0, keepdims=True)
        vals.append(best)
        idxs.append(sel)
        logits = jnp.where(e_iota == sel, -jnp.inf, logits)
    exps = [jnp.exp(v - vals[0]) for v in vals]
    total = exps[0] + exps[1] + exps[2] + exps[3]
    idx_ref[...] = jnp.concatenate(idxs, axis=0)
    gate_ref[...] = jnp.concatenate([e / total for e in exps], axis=0)


def _router(x, wt, b, tm):
    m, d = x.shape
    n_e = wt.shape[0]
    return pl.pallas_call(
        _router_body,
        grid=(m // tm,),
        in_specs=[pl.BlockSpec((tm, d), lambda i: (i, 0)),
                  pl.BlockSpec((n_e, d), lambda i: (0, 0)),
                  pl.BlockSpec((n_e, 1), lambda i: (0, 0))],
        out_specs=[pl.BlockSpec((TOP_K, tm), lambda i: (0, i)),
                   pl.BlockSpec((TOP_K, tm), lambda i: (0, i))],
        out_shape=[jax.ShapeDtypeStruct((TOP_K, m), jnp.int32),
                   jax.ShapeDtypeStruct((TOP_K, m), F32)],
        compiler_params=_params("parallel"),
        name="router_topk",
    )(x, wt, b.reshape(n_e, 1))


def _dispatch_tables(idx_t, gate_t, n_experts):
    n_tok = idx_t.shape[1]
    n_asg = n_tok * TOP_K
    n_super = n_asg // MOE_SUPER + n_experts
    n_slots = n_super * MOE_SUPER
    flat_e = idx_t.T.reshape(n_asg)
    flat_w = gate_t.T.reshape(n_asg)
    onehot = (flat_e[:, None] == jnp.arange(n_experts, dtype=jnp.int32)[None, :]).astype(jnp.int32)
    csum = jnp.cumsum(onehot, axis=0)
    counts = csum[-1]
    rank = jnp.sum(csum * onehot, axis=1) - 1
    n_sb = (counts + MOE_SUPER - 1) // MOE_SUPER
    sb_end = jnp.cumsum(n_sb)
    sb_start = sb_end - n_sb
    pos = (sb_start * MOE_SUPER)[flat_e] + rank
    asg = jnp.arange(n_asg, dtype=jnp.int32)
    slot_tok = jnp.zeros((n_slots,), jnp.int32).at[pos].set(asg // TOP_K)
    slot_w = jnp.zeros((n_slots,), F32).at[pos].set(flat_w)
    n_used = sb_end[-1]
    s = jnp.arange(n_super, dtype=jnp.int32)
    used = s < n_used
    s_eff = jnp.minimum(s, n_used - 1)
    sb_e = jnp.clip(jnp.searchsorted(sb_end, s_eff, side="right"), 0, n_experts - 1).astype(jnp.int32)
    rows = jnp.clip(counts[sb_e] - (s_eff - sb_start[sb_e]) * MOE_SUPER, 0, MOE_SUPER)
    sb_rows = jnp.where(used, rows, 0).astype(jnp.int32)
    return pos.astype(jnp.int32), slot_tok, slot_w, sb_e, sb_rows, s_eff.astype(jnp.int32), n_super


def _gather_body(rows_ref, tok_ref, x_hbm, o_ref, buf_ref, sem):
    j = pl.program_id(0)
    per = MOE_SUPER // MOE_SUB
    valid = rows_ref[j // per] - (j % per) * MOE_SUB

    def row_copy(r):
        return pltpu.make_async_copy(x_hbm.at[pl.ds(tok_ref[0, r], 1)], buf_ref.at[pl.ds(r, 1)], sem)

    @pl.when(valid > 0)
    def _():
        def start(r, carry):
            row_copy(r).start()
            return carry

        def wait(r, carry):
            row_copy(r).wait()
            return carry

        lax.fori_loop(0, MOE_SUB, start, 0)
        lax.fori_loop(0, MOE_SUB, wait, 0)
        o_ref[...] = buf_ref[...].astype(o_ref.dtype)

    @pl.when(valid <= 0)
    def _():
        o_ref[...] = jnp.zeros_like(o_ref)


def _dispatch_gather(x, slot_tok, sb_rows, n_super):
    d = x.shape[1]
    per = MOE_SUPER // MOE_SUB
    n_blk = n_super * per
    return pl.pallas_call(
        _gather_body,
        grid_spec=pltpu.PrefetchScalarGridSpec(
            num_scalar_prefetch=1,
            grid=(n_blk,),
            in_specs=[pl.BlockSpec((None, 1, MOE_SUB), lambda j, rows: (j, 0, 0),
                                   memory_space=pltpu.SMEM),
                      pl.BlockSpec(memory_space=pl.ANY)],
            out_specs=pl.BlockSpec((MOE_SUB, d), lambda j, rows: (j, 0)),
            scratch_shapes=[pltpu.VMEM((MOE_SUB, d), F32), pltpu.SemaphoreType.DMA(())],
        ),
        out_shape=jax.ShapeDtypeStruct((n_super * MOE_SUPER, d), BF16),
        compiler_params=_params("arbitrary"),
        name="moe_gather",
    )(sb_rows, slot_tok.reshape(n_blk, 1, MOE_SUB), x)


def _expert_body(e_ref, rows_ref, src_ref, x_ref, wg_ref, wu_ref, wd_ref, bg_ref, bu_ref, bd_ref,
                 sw_ref, o_ref):
    s, j = pl.program_id(0), pl.program_id(1)
    last = pl.num_programs(1) - 1
    rows = rows_ref[s]
    for sub in range(MOE_SUPER // MOE_SUB):
        sl = pl.ds(sub * MOE_SUB, MOE_SUB)

        @pl.when(sub * MOE_SUB < rows)
        def _():
            x = x_ref[sl, :]
            gate = jnp.minimum(_dot(x, wg_ref[...]) + bg_ref[...], SWIGLU_LIMIT)
            up = jnp.clip(_dot(x, wu_ref[...]) + bu_ref[...], -SWIGLU_LIMIT, SWIGLU_LIMIT)
            act = (up + 1.0) * (gate / (1.0 + jnp.exp(-SWIGLU_ALPHA * gate)))
            y = _dot(act.astype(BF16), wd_ref[...])

            @pl.when(j == 0)
            def _():
                o_ref[sl, :] = y + bd_ref[...]

            @pl.when(j > 0)
            def _():
                o_ref[sl, :] += y

            @pl.when(j == last)
            def _():
                o_ref[sl, :] *= sw_ref[sl, :]

        @pl.when((sub * MOE_SUB >= rows) & (j == 0))
        def _():
            o_ref[sl, :] = jnp.zeros((MOE_SUB, o_ref.shape[1]), o_ref.dtype)


def _experts(xb, wg, wu, wd, bg, bu, bd, slot_w, sb_e, sb_rows, sb_src, n_super, tf):
    d = xb.shape[1]
    n_e, _, f = wg.shape
    nf = f // tf

    def jeff(s, j, rows):
        return jnp.where(rows[s] > 0, j, nf - 1)

    return pl.pallas_call(
        _expert_body,
        grid_spec=pltpu.PrefetchScalarGridSpec(
            num_scalar_prefetch=3,
            grid=(n_super, nf),
            in_specs=[
                pl.BlockSpec((MOE_SUPER, d), lambda s, j, e, rows, src: (src[s], 0)),
                pl.BlockSpec((None, d, tf), lambda s, j, e, rows, src: (e[s], 0, jeff(s, j, rows))),
                pl.BlockSpec((None, d, tf), lambda s, j, e, rows, src: (e[s], 0, jeff(s, j, rows))),
                pl.BlockSpec((None, tf, d), lambda s, j, e, rows, src: (e[s], jeff(s, j, rows), 0)),
                pl.BlockSpec((None, 1, tf), lambda s, j, e, rows, src: (e[s], 0, jeff(s, j, rows))),
                pl.BlockSpec((None, 1, tf), lambda s, j, e, rows, src: (e[s], 0, jeff(s, j, rows))),
                pl.BlockSpec((None, 1, d), lambda s, j, e, rows, src: (e[s], 0, 0)),
                pl.BlockSpec((MOE_SUPER, 1), lambda s, j, e, rows, src: (src[s], 0)),
            ],
            out_specs=pl.BlockSpec((MOE_SUPER, d), lambda s, j, e, rows, src: (s, 0)),
        ),
        out_shape=jax.ShapeDtypeStruct((n_super * MOE_SUPER, d), F32),
        compiler_params=_params("arbitrary", "arbitrary"),
        name="moe_experts",
    )(sb_e, sb_rows, sb_src, xb, wg, wu, wd, bg.reshape(n_e, 1, f), bu.reshape(n_e, 1, f),
      bd.reshape(n_e, 1, d), slot_w.reshape(-1, 1))


def _combine_body(pos_ref, x_ref, y_hbm, g_ref, b_ref, o_ref, buf_ref, sem, *, alpha):
    tm = x_ref.shape[0]

    def row_copy(a):
        t, k = a // TOP_K, a % TOP_K
        return pltpu.make_async_copy(y_hbm.at[pl.ds(pos_ref[0, a], 1)], buf_ref.at[k, pl.ds(t, 1)], sem)

    def start(a, carry):
        row_copy(a).start()
        return carry

    def wait(a, carry):
        row_copy(a).wait()
        return carry

    lax.fori_loop(0, tm * TOP_K, start, 0)
    lax.fori_loop(0, tm * TOP_K, wait, 0)
    y = (buf_ref[0] + buf_ref[1]) + (buf_ref[2] + buf_ref[3])
    o_ref[...] = _layer_norm_rows(alpha * x_ref[...] + y, g_ref[...], b_ref[...])


def _combine_ln(x, yb, pos, g, b, alpha, tm):
    m, d = x.shape
    nt = m // tm
    return pl.pallas_call(
        functools.partial(_combine_body, alpha=alpha),
        grid=(nt,),
        in_specs=[pl.BlockSpec((None, 1, tm * TOP_K), lambda i: (i, 0, 0), memory_space=pltpu.SMEM),
                  pl.BlockSpec((tm, d), lambda i: (i, 0)),
                  pl.BlockSpec(memory_space=pl.ANY),
                  pl.BlockSpec((1, d), lambda i: (0, 0)), pl.BlockSpec((1, d), lambda i: (0, 0))],
        out_specs=pl.BlockSpec((tm, d), lambda i: (i, 0)),
        out_shape=jax.ShapeDtypeStruct((m, d), F32),
        scratch_shapes=[pltpu.VMEM((TOP_K, tm, d), F32), pltpu.SemaphoreType.DMA(())],
        compiler_params=_params("arbitrary"),
        name="moe_combine_ln",
    )(pos.reshape(nt, 1, tm * TOP_K), x, yb, g.reshape(1, d), b.reshape(1, d))


def _mixer(x, w_in, conv_w, w_gate2, b_gate2, gla_norm_g, w_out, ln_g, ln_b, alpha, bsz, seq):
    d = x.shape[1]
    ch = conv_w.shape[1]
    dk_total = w_gate2.shape[1]
    dk = dk_total // GLA_HEADS
    dv = gla_norm_g.shape[1]
    main_cols = w_in.shape[1] - GLA_GATE_RANK
    proj = _matmul(x, w_in[:, :main_cols].astype(BF16), tm=1024, tn=512, name="mixer_in_proj")
    w1p = jnp.zeros((d, LANE), BF16).at[:, :GLA_GATE_RANK].set(w_in[:, main_cols:].astype(BF16))
    w2p = jnp.zeros((LANE, dk_total), BF16).at[:GLA_GATE_RANK, :].set(w_gate2.astype(BF16))
    log_a = _gla_gate(x, w1p, w2p, b_gate2.reshape(1, dk_total), tm=1024)
    y_conv = _short_conv(proj, conv_w, bsz, seq, ch, tt=512, cw=512)
    q_off = 3 * ch
    k_off = q_off + dk_total
    v_off = k_off + dk_total
    r_off = v_off + GLA_HEADS * dv
    y_gla = _gla(proj, log_a, gla_norm_g, bsz, seq, q_off, k_off, v_off, r_off, dk, dv, tt=512)
    w_out_bf = w_out.astype(BF16)
    return _proj_ln(y_conv, y_gla, w_out_bf[:ch], w_out_bf[ch:], x, ln_g, ln_b, alpha, tm=512)


def _moe(x, w_router, b_router, w_gu, b_gu, w_down, b_down, ln_g, ln_b, alpha):
    n_e = w_router.shape[1]
    idx_t, gate_t = _router(x, w_router.T.astype(BF16), b_router, tm=1024)
    pos, slot_tok, slot_w, sb_e, sb_rows, sb_src, n_super = _dispatch_tables(idx_t, gate_t, n_e)
    xb = _dispatch_gather(x, slot_tok, sb_rows, n_super)
    wg = w_gu[:, :, 0::2].astype(BF16)
    wu = w_gu[:, :, 1::2].astype(BF16)
    yb = _experts(xb, wg, wu, w_down.astype(BF16), b_gu[:, 0::2], b_gu[:, 1::2], b_down,
                  slot_w, sb_e, sb_rows, sb_src, n_super, tf=512)
    return _combine_ln(x, yb, pos, ln_g, ln_b, alpha, tm=128)


def kernel(x, mem, w_in, conv_w, w_gate2, b_gate2, gla_norm_g, w_out, ln_mix_g, ln_mix_b, w_xq, w_xkv, w_xo, ln_xa_g, ln_xa_b, w_router, b_router, w_gu, b_gu, w_down, b_down, ln_moe_g, ln_moe_b):
    bsz, seq, d = x.shape
    depth = w_in.shape[0]
    alpha = (2 * depth) ** 0.25
    xf = x.reshape(bsz * seq, d)
    memf = mem.reshape(bsz * mem.shape[1], d)
    for l in range(depth):
        xf = _mixer(xf, w_in[l], conv_w[l], w_gate2[l], b_gate2[l], gla_norm_g[l], w_out[l],
                    ln_mix_g[l], ln_mix_b[l], alpha, bsz, seq)
        kv = _matmul(memf, w_xkv[l].astype(BF16), tm=memf.shape[0], tn=512, name="xattn_kv_proj")
        xf = _cross_attn(xf, kv, w_xq[l].astype(BF16), w_xo[l].astype(BF16), ln_xa_g[l], ln_xa_b[l],
                         alpha, bsz, seq, tm=512)
        xf = _moe(xf, w_router[l], b_router[l], w_gu[l], b_gu[l], w_down[l], b_down[l],
                  ln_moe_g[l], ln_moe_b[l], alpha)
    return xf.reshape(bsz, seq, d)
```

```python
import functools

import numpy as np
import jax
import jax.numpy as jnp
from jax import lax
from jax.experimental import pallas as pl
from jax.experimental.pallas import tpu as pltpu

CONV_W = 3
GLA_HEADS = 4
GLA_GATE_RANK = 16
GLA_TAU = 16.0
XA_HEADS = 4
TOP_K = 4
SWIGLU_LIMIT = 7.0
SWIGLU_ALPHA = 1.702
LN_EPS = 1e-5
NORM_EPS = 1e-6

LANE = 128
GLA_CHUNK = 128
GLA_LEVELS = 7
MOE_SUPER = 1024
MOE_SUB = 256
VMEM_LIMIT = 56 * 1024 * 1024

BF16 = jnp.bfloat16
F32 = jnp.float32


def _dot(a, b):
    return jnp.dot(a, b, preferred_element_type=F32)


def _dot_nt(a, b):
    return lax.dot_general(a, b, (((1,), (1,)), ((), ())), preferred_element_type=F32)


def _dot_tn(a, b):
    return lax.dot_general(a, b, (((0,), (0,)), ((), ())), preferred_element_type=F32)


def _params(*sem):
    return pltpu.CompilerParams(dimension_semantics=sem, vmem_limit_bytes=VMEM_LIMIT)


def _layer_norm_rows(y, g, b):
    mu = jnp.mean(y, axis=-1, keepdims=True)
    yc = y - mu
    var = jnp.mean(yc * yc, axis=-1, keepdims=True)
    return yc * lax.rsqrt(var + LN_EPS) * g + b


def _matmul_body(a_ref, w_ref, o_ref, abf_ref):
    @pl.when(pl.program_id(1) == 0)
    def _():
        abf_ref[...] = a_ref[...].astype(BF16)

    o_ref[...] = _dot(abf_ref[...], w_ref[...]).astype(o_ref.dtype)


def _matmul(a, w, tm, tn, name):
    m, k = a.shape
    n = w.shape[1]
    return pl.pallas_call(
        _matmul_body,
        grid=(m // tm, n // tn),
        in_specs=[pl.BlockSpec((tm, k), lambda i, j: (i, 0)),
                  pl.BlockSpec((k, tn), lambda i, j: (0, j))],
        out_specs=pl.BlockSpec((tm, tn), lambda i, j: (i, j)),
        out_shape=jax.ShapeDtypeStruct((m, n), F32),
        scratch_shapes=[pltpu.VMEM((tm, k), BF16)],
        compiler_params=_params("parallel", "arbitrary"),
        name=name,
    )(a, w)


def _gate_body(x_ref, w1_ref, w2_ref, b_ref, o_ref):
    low = _dot(x_ref[...].astype(BF16), w1_ref[...])
    z = _dot(low.astype(BF16), w2_ref[...]) + b_ref[...]
    log_sig = jnp.minimum(z, 0.0) - jnp.log1p(jnp.exp(-jnp.abs(z)))
    o_ref[...] = log_sig / GLA_TAU


def _gla_gate(x, w1p, w2p, b2, tm):
    m, d = x.shape
    n = w2p.shape[1]
    return pl.pallas_call(
        _gate_body,
        grid=(m // tm,),
        in_specs=[pl.BlockSpec((tm, d), lambda i: (i, 0)),
                  pl.BlockSpec(w1p.shape, lambda i: (0, 0)),
                  pl.BlockSpec(w2p.shape, lambda i: (0, 0)),
                  pl.BlockSpec((1, n), lambda i: (0, 0))],
        out_specs=pl.BlockSpec((tm, n), lambda i: (i, 0)),
        out_shape=jax.ShapeDtypeStruct((m, n), F32),
        compiler_params=_params("parallel"),
        name="gla_gate",
    )(x, w1p, w2p, b2)


def _conv_body(gc_ref, gb_ref, h_ref, gch_ref, hh_ref, w_ref, o_ref):
    tt = gc_ref.shape[0]
    u = gc_ref[...] * h_ref[...]
    uh = jnp.where(pl.program_id(1) == 0, 0.0, gch_ref[...] * hh_ref[...])
    row = lax.broadcasted_iota(jnp.int32, u.shape, 0)
    u1 = jnp.where(row == 0, uh[7:8, :], pltpu.roll(u, 1, 0))
    u2 = pltpu.roll(u, 2, 0)
    u2 = jnp.where(row == 0, uh[6:7, :], jnp.where(row == 1, uh[7:8, :], u2))
    w = w_ref[...]
    conv = w[0:1, :] * u2 + w[1:2, :] * u1 + w[2:3, :] * u
    o_ref[...] = (gb_ref[...] * conv).astype(o_ref.dtype)
    del tt


def _short_conv(proj, conv_w, bsz, seq, ch, tt, cw):
    nt, nc = seq // tt, ch // cw

    def main(off):
        return pl.BlockSpec((tt, cw), lambda b, i, c: (b * nt + i, off * nc + c))

    def halo(off):
        return pl.BlockSpec(
            (8, cw), lambda b, i, c: (jnp.maximum((b * seq + i * tt) // 8 - 1, 0), off * nc + c))

    return pl.pallas_call(
        _conv_body,
        grid=(bsz, nt, nc),
        in_specs=[main(0), main(1), main(2), halo(0), halo(2),
                  pl.BlockSpec((CONV_W, cw), lambda b, i, c: (0, c))],
        out_specs=pl.BlockSpec((tt, cw), lambda b, i, c: (b * nt + i, c)),
        out_shape=jax.ShapeDtypeStruct((bsz * seq, ch), BF16),
        compiler_params=_params("parallel", "parallel", "parallel"),
        name="short_conv",
    )(proj, proj, proj, proj, proj, conv_w)


def _gla_tables():
    c = GLA_CHUNK
    i = np.arange(c)[:, None]
    t = np.arange(c)[None, :]
    mats = []
    for lvl in range(GLA_LEVELS):
        m = ((i >> (lvl + 1)) << (lvl + 1)) + (1 << lvl) - 1
        upper = (i > m) & (t > m) & (t <= i)
        lower = (i <= m) & (t > i) & (t <= m)
        mats.append(upper | lower)
    mats.append(t <= i)
    return jnp.asarray(np.concatenate(mats, axis=0).astype(np.float32), dtype=BF16)


def _gla_body(q_ref, k_ref, g_ref, v_ref, r_ref, gn_ref, tab_ref, o_ref, st_ref, *, dk_scale):
    c = GLA_CHUNK
    dk = q_ref.shape[1]

    @pl.when(pl.program_id(2) == 0)
    def _():
        st_ref[...] = jnp.zeros_like(st_ref)

    ii = lax.broadcasted_iota(jnp.int32, (c, c), 0)
    jj = lax.broadcasted_iota(jnp.int32, (c, c), 1)
    xor = ii ^ jj
    lower = ii > jj

    for ch in range(q_ref.shape[0] // c):
        rows = pl.ds(ch * c, c)
        q = q_ref[rows, :] * dk_scale
        k = k_ref[rows, :]
        g = g_ref[rows, :]
        v_bf = v_ref[rows, :].astype(BF16)
        g_hi = g.astype(BF16)
        g_lo = (g - g_hi.astype(F32)).astype(BF16)
        sums = _dot(tab_ref[...], jnp.concatenate([g_hi, g_lo], axis=1))
        sums = sums[:, :dk] + sums[:, dk:]

        scores = jnp.where(ii == jj, _dot_nt(q.astype(BF16), k.astype(BF16)), 0.0)
        for lvl in range(GLA_LEVELS):
            w = jnp.exp(sums[lvl * c:(lvl + 1) * c, :])
            s_l = _dot_nt((q * w).astype(BF16), (k * w).astype(BF16))
            scores = jnp.where(((xor >> lvl) == 1) & lower, s_l, scores)

        b = sums[GLA_LEVELS * c:, :]
        b_last = b[c - 1:c, :]
        st = st_ref[...]
        o = _dot_nt((q * jnp.exp(b)).astype(BF16), st.astype(BF16)) + _dot(scores.astype(BF16), v_bf)
        k_dec = (k * jnp.exp(b_last - b)).astype(BF16)
        st_ref[...] = st * jnp.exp(b_last) + _dot_tn(v_bf, k_dec)

        o = o * lax.rsqrt(jnp.mean(o * o, axis=-1, keepdims=True) + NORM_EPS) * gn_ref[...]
        r = r_ref[rows, :]
        o_ref[rows, :] = (o * (r / (1.0 + jnp.exp(-r)))).astype(o_ref.dtype)


def _gla(proj, log_a, gn, bsz, seq, q_off, k_off, v_off, r_off, dk, dv, tt):
    nt = seq // tt
    tab = _gla_tables()

    def spec(width, off):
        return pl.BlockSpec((tt, width), lambda b, h, i: (b * nt + i, off // width + h))

    return pl.pallas_call(
        functools.partial(_gla_body, dk_scale=dk ** -0.5),
        grid=(bsz, GLA_HEADS, nt),
        in_specs=[spec(dk, q_off), spec(dk, k_off), spec(dk, 0), spec(dv, v_off), spec(dv, r_off),
                  pl.BlockSpec((None, 1, dv), lambda b, h, i: (h, 0, 0)),
                  pl.BlockSpec(tab.shape, lambda b, h, i: (0, 0))],
        out_specs=pl.BlockSpec((tt, dv), lambda b, h, i: (b * nt + i, h)),
        out_shape=jax.ShapeDtypeStruct((bsz * seq, GLA_HEADS * dv), BF16),
        scratch_shapes=[pltpu.VMEM((dv, dk), F32)],
        compiler_params=_params("parallel", "parallel", "arbitrary"),
        name="gla",
    )(proj, proj, log_a, proj, proj, gn.reshape(GLA_HEADS, 1, dv), tab)


def _proj_ln_body(a0_ref, a1_ref, w0_ref, w1_ref, x_ref, g_ref, b_ref, o_ref, *, alpha):
    y = _dot(a0_ref[...], w0_ref[...]) + _dot(a1_ref[...], w1_ref[...])
    o_ref[...] = _layer_norm_rows(alpha * x_ref[...] + y, g_ref[...], b_ref[...])


def _proj_ln(a0, a1, w0, w1, x, g, b, alpha, tm):
    m, d = x.shape
    row = lambda i: (i, 0)
    fixed = lambda i: (0, 0)
    return pl.pallas_call(
        functools.partial(_proj_ln_body, alpha=alpha),
        grid=(m // tm,),
        in_specs=[pl.BlockSpec((tm, a0.shape[1]), row), pl.BlockSpec((tm, a1.shape[1]), row),
                  pl.BlockSpec(w0.shape, fixed), pl.BlockSpec(w1.shape, fixed),
                  pl.BlockSpec((tm, d), row), pl.BlockSpec((1, d), fixed), pl.BlockSpec((1, d), fixed)],
        out_specs=pl.BlockSpec((tm, d), row),
        out_shape=jax.ShapeDtypeStruct((m, d), F32),
        compiler_params=_params("parallel"),
        name="mixer_out_ln",
    )(a0, a1, w0, w1, x, g.reshape(1, d), b.reshape(1, d))


def _xattn_body(x_ref, kv_ref, wq_ref, wo_ref, g_ref, b_ref, o_ref, *, alpha):
    x = x_ref[...]
    xa = wq_ref.shape[1]
    hd = xa // XA_HEADS
    q = _dot(x.astype(BF16), wq_ref[...])
    heads = []
    for h in range(XA_HEADS):
        qh = q[:, h * hd:(h + 1) * hd].astype(BF16)
        kh = kv_ref[:, h * hd:(h + 1) * hd].astype(BF16)
        vh = kv_ref[:, xa + h * hd:xa + (h + 1) * hd].astype(BF16)
        s = _dot_nt(qh, kh) * hd ** -0.5
        p = jnp.exp(s - jnp.max(s, axis=-1, keepdims=True))
        p = p / jnp.sum(p, axis=-1, keepdims=True)
        heads.append(_dot(p.astype(BF16), vh))
    o = jnp.concatenate(heads, axis=1).astype(BF16)
    y = _dot(o, wo_ref[...])
    o_ref[...] = _layer_norm_rows(alpha * x + y, g_ref[...], b_ref[...])


def _cross_attn(x, kv, wq, wo, g, b, alpha, bsz, seq, tm):
    m, d = x.shape
    nt = seq // tm
    mem_len = kv.shape[0] // bsz
    fixed = lambda bb, i: (0, 0)
    return pl.pallas_call(
        functools.partial(_xattn_body, alpha=alpha),
        grid=(bsz, nt),
        in_specs=[pl.BlockSpec((tm, d), lambda bb, i: (bb * nt + i, 0)),
                  pl.BlockSpec((mem_len, kv.shape[1]), lambda bb, i: (bb, 0)),
                  pl.BlockSpec(wq.shape, fixed), pl.BlockSpec(wo.shape, fixed),
                  pl.BlockSpec((1, d), fixed), pl.BlockSpec((1, d), fixed)],
        out_specs=pl.BlockSpec((tm, d), lambda bb, i: (bb * nt + i, 0)),
        out_shape=jax.ShapeDtypeStruct((m, d), F32),
        compiler_params=_params("parallel", "parallel"),
        name="cross_attn_ln",
    )(x, kv, wq, wo, g.reshape(1, d), b.reshape(1, d))


def _router_body(x_ref, wt_ref, b_ref, idx_ref, gate_ref):
    logits = _dot_nt(wt_ref[...], x_ref[...].astype(BF16)) + b_ref[...]
    n_e = logits.shape[0]
    e_iota = lax.broadcasted_iota(jnp.int32, logits.shape, 0)
    vals, idxs = [], []
    for _ in range(TOP_K):
        best = jnp.max(logits, axis=0, keepdims=True)
        sel = jnp.min(jnp.where(logits == best, e_iota, n_e), axis=0, keepdims=True)
        vals.append(best)
        idxs.append(sel)
        logits = jnp.where(e_iota == sel, -jnp.inf, logits)
    exps = [jnp.exp(v - vals[0]) for v in vals]
    total = exps[0] + exps[1] + exps[2] + exps[3]
    idx_ref[...] = jnp.concatenate(idxs, axis=0)
    gate_ref[...] = jnp.concatenate([e / total for e in exps], axis=0)


def _router(x, wt, b, tm):
    m, d = x.shape
    n_e = wt.shape[0]
    return pl.pallas_call(
        _router_body,
        grid=(m // tm,),
        in_specs=[pl.BlockSpec((tm, d), lambda i: (i, 0)),
                  pl.BlockSpec((n_e, d), lambda i: (0, 0)),
                  pl.BlockSpec((n_e, 1), lambda i: (0, 0))],
        out_specs=[pl.BlockSpec((TOP_K, tm), lambda i: (0, i)),
                   pl.BlockSpec((TOP_K, tm), lambda i: (0, i))],
        out_shape=[jax.ShapeDtypeStruct((TOP_K, m), jnp.int32),
                   jax.ShapeDtypeStruct((TOP_K, m), F32)],
        compiler_params=_params("parallel"),
        name="router_topk",
    )(x, wt, b.reshape(n_e, 1))


def _dispatch_tables(idx_t, gate_t, n_experts):
    n_tok = idx_t.shape[1]
    n_asg = n_tok * TOP_K
    n_super = n_asg // MOE_SUPER + n_experts
    n_slots = n_super * MOE_SUPER
    flat_e = idx_t.T.reshape(n_asg)
    flat_w = gate_t.T.reshape(n_asg)
    onehot = (flat_e[:, None] == jnp.arange(n_experts, dtype=jnp.int32)[None, :]).astype(jnp.int32)
    csum = jnp.cumsum(onehot, axis=0)
    counts = csum[-1]
    rank = jnp.sum(csum * onehot, axis=1) - 1
    n_sb = (counts + MOE_SUPER - 1) // MOE_SUPER
    sb_end = jnp.cumsum(n_sb)
    sb_start = sb_end - n_sb
    pos = (sb_start * MOE_SUPER)[flat_e] + rank
    asg = jnp.arange(n_asg, dtype=jnp.int32)
    slot_tok = jnp.zeros((n_slots,), jnp.int32).at[pos].set(asg // TOP_K)
    slot_w = jnp.zeros((n_slots,), F32).at[pos].set(flat_w)
    n_used = sb_end[-1]
    s = jnp.arange(n_super, dtype=jnp.int32)
    used = s < n_used
    s_eff = jnp.minimum(s, n_used - 1)
    sb_e = jnp.clip(jnp.searchsorted(sb_end, s_eff, side="right"), 0, n_experts - 1).astype(jnp.int32)
    rows = jnp.clip(counts[sb_e] - (s_eff - sb_start[sb_e]) * MOE_SUPER, 0, MOE_SUPER)
    sb_rows = jnp.where(used, rows, 0).astype(jnp.int32)
    return pos.astype(jnp.int32), slot_tok, slot_w, sb_e, sb_rows, s_eff.astype(jnp.int32), n_super


def _gather_body(rows_ref, tok_ref, x_hbm, o_ref, buf_ref, sem):
    j = pl.program_id(0)
    per = MOE_SUPER // MOE_SUB
    valid = rows_ref[j // per] - (j % per) * MOE_SUB

    def row_copy(r):
        return pltpu.make_async_copy(x_hbm.at[pl.ds(tok_ref[0, r], 1)], buf_ref.at[pl.ds(r, 1)], sem)

    @pl.when(valid > 0)
    def _():
        def start(r, carry):
            row_copy(r).start()
            return carry

        lax.fori_loop(0, MOE_SUB, start, 0, unroll=8)
        pltpu.make_async_copy(buf_ref, buf_ref, sem).wait()
        o_ref[...] = buf_ref[...].astype(o_ref.dtype)

    @pl.when(valid <= 0)
    def _():
        o_ref[...] = jnp.zeros_like(o_ref)


def _dispatch_gather(x, slot_tok, sb_rows, n_super):
    d = x.shape[1]
    per = MOE_SUPER // MOE_SUB
    n_blk = n_super * per
    return pl.pallas_call(
        _gather_body,
        grid_spec=pltpu.PrefetchScalarGridSpec(
            num_scalar_prefetch=1,
            grid=(n_blk,),
            in_specs=[pl.BlockSpec((None, 1, MOE_SUB), lambda j, rows: (j, 0, 0),
                                   memory_space=pltpu.SMEM),
                      pl.BlockSpec(memory_space=pl.ANY)],
            out_specs=pl.BlockSpec((MOE_SUB, d), lambda j, rows: (j, 0)),
            scratch_shapes=[pltpu.VMEM((MOE_SUB, d), F32), pltpu.SemaphoreType.DMA(())],
        ),
        out_shape=jax.ShapeDtypeStruct((n_super * MOE_SUPER, d), BF16),
        compiler_params=_params("arbitrary"),
        name="moe_gather",
    )(sb_rows, slot_tok.reshape(n_blk, 1, MOE_SUB), x)


def _expert_body(e_ref, rows_ref, src_ref, x_ref, wgu_ref, wd_ref, bgu_ref, bd_ref, sw_ref, o_ref,
                 wgu_bf, wd_perm, wd_bf):
    s, j = pl.program_id(0), pl.program_id(1)
    last = pl.num_programs(1) - 1
    rows = rows_ref[s]
    tf = wd_ref.shape[0]
    half = tf // 2

    @pl.when(rows > 0)
    def _():
        wgu_bf[...] = wgu_ref[...].astype(BF16)
        for c in range(wd_ref.shape[1] // LANE):
            cols = pl.ds(c * LANE, LANE)
            wd_perm[c, pl.ds(0, half, stride=2), :] = wd_ref[0:half, cols]
            wd_perm[c, pl.ds(1, half, stride=2), :] = wd_ref[half:tf, cols]
            wd_bf[:, cols] = wd_perm[c].astype(BF16)

    even_lane = lax.broadcasted_iota(jnp.int32, (1, tf), 1) % 2 == 0

    @pl.when(j == 0)
    def _():
        o_ref[...] = jnp.broadcast_to(bd_ref[...], o_ref.shape)

    def run(n_sub):
        for sub in range(n_sub):
            sl = pl.ds(sub * MOE_SUB, MOE_SUB)
            gu = _dot(x_ref[sl, :], wgu_bf[...]) + bgu_ref[...]
            up = jnp.clip(pltpu.roll(gu, 2 * tf - 1, 1), -SWIGLU_LIMIT, SWIGLU_LIMIT)
            gate = jnp.minimum(gu, SWIGLU_LIMIT)
            act = (up + 1.0) * (gate / (1.0 + jnp.exp(-SWIGLU_ALPHA * gate)))
            packed = jnp.where(even_lane, act[:, :tf], pltpu.roll(act[:, tf:], 1, 1))
            o_ref[sl, :] += _dot(packed.astype(BF16), wd_bf[...])

    half_subs = MOE_SUPER // MOE_SUB // 2

    @pl.when((rows > 0) & (rows <= half_subs * MOE_SUB))
    def _():
        run(half_subs)

    @pl.when(rows > half_subs * MOE_SUB)
    def _():
        run(2 * half_subs)

    @pl.when(j == last)
    def _():
        o_ref[...] *= sw_ref[...]


def _experts(xb, w_gu, w_down, b_gu, b_down, layer, slot_w, sb_e, sb_rows, sb_src, n_super, tf):
    d = xb.shape[1]
    n_l, n_e, f, _ = w_down.shape
    nf = f // tf

    def jeff(s, j, rows):
        return jnp.where(rows[s] > 0, j, nf - 1)

    return pl.pallas_call(
        _expert_body,
        grid_spec=pltpu.PrefetchScalarGridSpec(
            num_scalar_prefetch=3,
            grid=(n_super, nf),
            in_specs=[
                pl.BlockSpec((MOE_SUPER, d), lambda s, j, e, rows, src: (src[s], 0)),
                pl.BlockSpec((None, None, d, 2 * tf),
                             lambda s, j, e, rows, src: (layer, e[s], 0, jeff(s, j, rows))),
                pl.BlockSpec((None, None, tf, d),
                             lambda s, j, e, rows, src: (layer, e[s], jeff(s, j, rows), 0)),
                pl.BlockSpec((None, None, 1, 2 * tf),
                             lambda s, j, e, rows, src: (layer, e[s], 0, jeff(s, j, rows))),
                pl.BlockSpec((None, None, 1, d), lambda s, j, e, rows, src: (layer, e[s], 0, 0)),
                pl.BlockSpec((MOE_SUPER, 1), lambda s, j, e, rows, src: (src[s], 0)),
            ],
            out_specs=pl.BlockSpec((MOE_SUPER, d), lambda s, j, e, rows, src: (s, 0)),
            scratch_shapes=[pltpu.VMEM((d, 2 * tf), BF16), pltpu.VMEM((d // LANE, tf, LANE), F32),
                            pltpu.VMEM((tf, d), BF16)],
        ),
        out_shape=jax.ShapeDtypeStruct((n_super * MOE_SUPER, d), F32),
        compiler_params=_params("arbitrary", "arbitrary"),
        name="moe_experts",
    )(sb_e, sb_rows, sb_src, xb, w_gu, w_down, b_gu.reshape(n_l, n_e, 1, 2 * f),
      b_down.reshape(n_l, n_e, 1, d), slot_w.reshape(-1, 1))


def _combine_body(pos_ref, x_ref, y_hbm, g_ref, b_ref, o_ref, buf_ref, sem, *, alpha):
    tm = x_ref.shape[0]

    def row_copy(a):
        t, k = a // TOP_K, a % TOP_K
        return pltpu.make_async_copy(y_hbm.at[pl.ds(pos_ref[0, a], 1)], buf_ref.at[k, pl.ds(t, 1)], sem)

    def start(a, carry):
        row_copy(a).start()
        return carry

    lax.fori_loop(0, tm * TOP_K, start, 0, unroll=8)
    pltpu.make_async_copy(buf_ref, buf_ref, sem).wait()
    y = (buf_ref[0] + buf_ref[1]) + (buf_ref[2] + buf_ref[3])
    o_ref[...] = _layer_norm_rows(alpha * x_ref[...] + y, g_ref[...], b_ref[...])


def _combine_ln(x, yb, pos, g, b, alpha, tm):
    m, d = x.shape
    nt = m // tm
    return pl.pallas_call(
        functools.partial(_combine_body, alpha=alpha),
        grid=(nt,),
        in_specs=[pl.BlockSpec((None, 1, tm * TOP_K), lambda i: (i, 0, 0), memory_space=pltpu.SMEM),
                  pl.BlockSpec((tm, d), lambda i: (i, 0)),
                  pl.BlockSpec(memory_space=pl.ANY),
                  pl.BlockSpec((1, d), lambda i: (0, 0)), pl.BlockSpec((1, d), lambda i: (0, 0))],
        out_specs=pl.BlockSpec((tm, d), lambda i: (i, 0)),
        out_shape=jax.ShapeDtypeStruct((m, d), F32),
        scratch_shapes=[pltpu.VMEM((TOP_K, tm, d), F32), pltpu.SemaphoreType.DMA(())],
        compiler_params=_params("arbitrary"),
        name="moe_combine_ln",
    )(pos.reshape(nt, 1, tm * TOP_K), x, yb, g.reshape(1, d), b.reshape(1, d))


def _mixer(x, w_in, conv_w, w_gate2, b_gate2, gla_norm_g, w_out, ln_g, ln_b, alpha, bsz, seq):
    d = x.shape[1]
    ch = conv_w.shape[1]
    dk_total = w_gate2.shape[1]
    dk = dk_total // GLA_HEADS
    dv = gla_norm_g.shape[1]
    main_cols = w_in.shape[1] - GLA_GATE_RANK
    proj = _matmul(x, w_in[:, :main_cols].astype(BF16), tm=1024, tn=512, name="mixer_in_proj")
    w1p = jnp.zeros((d, LANE), BF16).at[:, :GLA_GATE_RANK].set(w_in[:, main_cols:].astype(BF16))
    w2p = jnp.zeros((LANE, dk_total), BF16).at[:GLA_GATE_RANK, :].set(w_gate2.astype(BF16))
    log_a = _gla_gate(x, w1p, w2p, b_gate2.reshape(1, dk_total), tm=1024)
    y_conv = _short_conv(proj, conv_w, bsz, seq, ch, tt=512, cw=512)
    q_off = 3 * ch
    k_off = q_off + dk_total
    v_off = k_off + dk_total
    r_off = v_off + GLA_HEADS * dv
    y_gla = _gla(proj, log_a, gla_norm_g, bsz, seq, q_off, k_off, v_off, r_off, dk, dv, tt=512)
    w_out_bf = w_out.astype(BF16)
    return _proj_ln(y_conv, y_gla, w_out_bf[:ch], w_out_bf[ch:], x, ln_g, ln_b, alpha, tm=512)


def _moe(x, layer, w_router, b_router, w_gu, b_gu, w_down, b_down, ln_g, ln_b, alpha):
    n_e = w_router.shape[1]
    idx_t, gate_t = _router(x, w_router.T.astype(BF16), b_router, tm=1024)
    pos, slot_tok, slot_w, sb_e, sb_rows, sb_src, n_super = _dispatch_tables(idx_t, gate_t, n_e)
    xb = _dispatch_gather(x, slot_tok, sb_rows, n_super)
    yb = _experts(xb, w_gu, w_down, b_gu, b_down, layer, slot_w, sb_e, sb_rows, sb_src, n_super, tf=256)
    return _combine_ln(x, yb, pos, ln_g, ln_b, alpha, tm=128)


def kernel(x, mem, w_in, conv_w, w_gate2, b_gate2, gla_norm_g, w_out, ln_mix_g, ln_mix_b, w_xq, w_xkv, w_xo, ln_xa_g, ln_xa_b, w_router, b_router, w_gu, b_gu, w_down, b_down, ln_moe_g, ln_moe_b):
    bsz, seq, d = x.shape
    depth = w_in.shape[0]
    alpha = (2 * depth) ** 0.25
    xf = x.reshape(bsz * seq, d)
    memf = mem.reshape(bsz * mem.shape[1], d)
    for l in range(depth):
        xf = _mixer(xf, w_in[l], conv_w[l], w_gate2[l], b_gate2[l], gla_norm_g[l], w_out[l],
                    ln_mix_g[l], ln_mix_b[l], alpha, bsz, seq)
        kv = _matmul(memf, w_xkv[l].astype(BF16), tm=memf.shape[0], tn=512, name="xattn_kv_proj")
        xf = _cross_attn(xf, kv, w_xq[l].astype(BF16), w_xo[l].astype(BF16), ln_xa_g[l], ln_xa_b[l],
                         alpha, bsz, seq, tm=512)
        xf = _moe(xf, l, w_router[l], b_router[l], w_gu, b_gu, w_down, b_down,
                  ln_moe_g[l], ln_moe_b[l], alpha)
    return xf.reshape(bsz, seq, d)
```

```python
import functools

import numpy as np
import jax
import jax.numpy as jnp
from jax import lax
from jax.experimental import pallas as pl
from jax.experimental.pallas import tpu as pltpu

CONV_W = 3
GLA_HEADS = 4
GLA_GATE_RANK = 16
GLA_TAU = 16.0
XA_HEADS = 4
TOP_K = 4
SWIGLU_LIMIT = 7.0
SWIGLU_ALPHA = 1.702
LN_EPS = 1e-5
NORM_EPS = 1e-6

LANE = 128
GLA_CHUNK = 128
GLA_LEVELS = 7
MOE_SUB = 256
MOE_PASS = 1536
VMEM_LIMIT = 56 * 1024 * 1024

BF16 = jnp.bfloat16
F32 = jnp.float32


def _dot(a, b):
    return jnp.dot(a, b, preferred_element_type=F32)


def _dot_nt(a, b):
    return lax.dot_general(a, b, (((1,), (1,)), ((), ())), preferred_element_type=F32)


def _dot_tn(a, b):
    return lax.dot_general(a, b, (((0,), (0,)), ((), ())), preferred_element_type=F32)


def _params(*sem):
    return pltpu.CompilerParams(dimension_semantics=sem, vmem_limit_bytes=VMEM_LIMIT)


def _layer_norm_rows(y, g, b):
    mu = jnp.mean(y, axis=-1, keepdims=True)
    yc = y - mu
    var = jnp.mean(yc * yc, axis=-1, keepdims=True)
    return yc * lax.rsqrt(var + LN_EPS) * g + b


def _matmul_body(a_ref, w_ref, o_ref, abf_ref):
    @pl.when(pl.program_id(1) == 0)
    def _():
        abf_ref[...] = a_ref[...].astype(BF16)

    o_ref[...] = _dot(abf_ref[...], w_ref[...]).astype(o_ref.dtype)


def _matmul(a, w, tm, tn, name):
    m, k = a.shape
    n = w.shape[1]
    return pl.pallas_call(
        _matmul_body,
        grid=(m // tm, n // tn),
        in_specs=[pl.BlockSpec((tm, k), lambda i, j: (i, 0)),
                  pl.BlockSpec((k, tn), lambda i, j: (0, j))],
        out_specs=pl.BlockSpec((tm, tn), lambda i, j: (i, j)),
        out_shape=jax.ShapeDtypeStruct((m, n), F32),
        scratch_shapes=[pltpu.VMEM((tm, k), BF16)],
        compiler_params=_params("parallel", "arbitrary"),
        name=name,
    )(a, w)


def _gate_body(x_ref, w1_ref, w2_ref, b_ref, o_ref):
    low = _dot(x_ref[...].astype(BF16), w1_ref[...])
    z = _dot(low.astype(BF16), w2_ref[...]) + b_ref[...]
    log_sig = jnp.minimum(z, 0.0) - jnp.log1p(jnp.exp(-jnp.abs(z)))
    o_ref[...] = log_sig / GLA_TAU


def _gla_gate(x, w1p, w2p, b2, tm):
    m, d = x.shape
    n = w2p.shape[1]
    return pl.pallas_call(
        _gate_body,
        grid=(m // tm,),
        in_specs=[pl.BlockSpec((tm, d), lambda i: (i, 0)),
                  pl.BlockSpec(w1p.shape, lambda i: (0, 0)),
                  pl.BlockSpec(w2p.shape, lambda i: (0, 0)),
                  pl.BlockSpec((1, n), lambda i: (0, 0))],
        out_specs=pl.BlockSpec((tm, n), lambda i: (i, 0)),
        out_shape=jax.ShapeDtypeStruct((m, n), F32),
        compiler_params=_params("parallel"),
        name="gla_gate",
    )(x, w1p, w2p, b2)


def _conv_body(gc_ref, gb_ref, h_ref, gch_ref, hh_ref, w_ref, o_ref):
    u = gc_ref[...] * h_ref[...]
    uh = jnp.where(pl.program_id(1) == 0, 0.0, gch_ref[...] * hh_ref[...])
    row = lax.broadcasted_iota(jnp.int32, u.shape, 0)
    u1 = jnp.where(row == 0, uh[7:8, :], pltpu.roll(u, 1, 0))
    u2 = pltpu.roll(u, 2, 0)
    u2 = jnp.where(row == 0, uh[6:7, :], jnp.where(row == 1, uh[7:8, :], u2))
    w = w_ref[...]
    conv = w[0:1, :] * u2 + w[1:2, :] * u1 + w[2:3, :] * u
    o_ref[...] = (gb_ref[...] * conv).astype(o_ref.dtype)


def _short_conv(proj, conv_w, bsz, seq, ch, tt, cw):
    nt, nc = seq // tt, ch // cw

    def main(off):
        return pl.BlockSpec((tt, cw), lambda b, i, c: (b * nt + i, off * nc + c))

    def halo(off):
        return pl.BlockSpec(
            (8, cw), lambda b, i, c: (jnp.maximum((b * seq + i * tt) // 8 - 1, 0), off * nc + c))

    return pl.pallas_call(
        _conv_body,
        grid=(bsz, nt, nc),
        in_specs=[main(0), main(1), main(2), halo(0), halo(2),
                  pl.BlockSpec((CONV_W, cw), lambda b, i, c: (0, c))],
        out_specs=pl.BlockSpec((tt, cw), lambda b, i, c: (b * nt + i, c)),
        out_shape=jax.ShapeDtypeStruct((bsz * seq, ch), BF16),
        compiler_params=_params("parallel", "parallel", "parallel"),
        name="short_conv",
    )(proj, proj, proj, proj, proj, conv_w)


def _gla_tables():
    c = GLA_CHUNK
    i = np.arange(c)[:, None]
    t = np.arange(c)[None, :]
    mats = []
    for lvl in range(GLA_LEVELS):
        m = ((i >> (lvl + 1)) << (lvl + 1)) + (1 << lvl) - 1
        upper = (i > m) & (t > m) & (t <= i)
        lower = (i <= m) & (t > i) & (t <= m)
        mats.append(upper | lower)
    mats.append(t <= i)
    return jnp.asarray(np.concatenate(mats, axis=0).astype(np.float32), dtype=BF16)


def _gla_body(q_ref, k_ref, g_ref, v_ref, r_ref, gn_ref, tab_ref, o_ref, st_ref, *, dk_scale):
    c = GLA_CHUNK
    dk = q_ref.shape[1]

    @pl.when(pl.program_id(2) == 0)
    def _():
        st_ref[...] = jnp.zeros_like(st_ref)

    ii = lax.broadcasted_iota(jnp.int32, (c, c), 0)
    jj = lax.broadcasted_iota(jnp.int32, (c, c), 1)
    xor = ii ^ jj
    lower = ii > jj

    for ch in range(q_ref.shape[0] // c):
        rows = pl.ds(ch * c, c)
        q = q_ref[rows, :] * dk_scale
        k = k_ref[rows, :]
        g = g_ref[rows, :]
        v_bf = v_ref[rows, :].astype(BF16)
        g_hi = g.astype(BF16)
        g_lo = (g - g_hi.astype(F32)).astype(BF16)
        sums = _dot(tab_ref[...], jnp.concatenate([g_hi, g_lo], axis=1))
        sums = sums[:, :dk] + sums[:, dk:]

        scores = jnp.where(ii == jj, _dot_nt(q.astype(BF16), k.astype(BF16)), 0.0)
        for lvl in range(GLA_LEVELS):
            w = jnp.exp(sums[lvl * c:(lvl + 1) * c, :])
            s_l = _dot_nt((q * w).astype(BF16), (k * w).astype(BF16))
            scores = jnp.where(((xor >> lvl) == 1) & lower, s_l, scores)

        b = sums[GLA_LEVELS * c:, :]
        b_last = b[c - 1:c, :]
        st = st_ref[...]
        o = _dot_nt((q * jnp.exp(b)).astype(BF16), st.astype(BF16)) + _dot(scores.astype(BF16), v_bf)
        k_dec = (k * jnp.exp(b_last - b)).astype(BF16)
        st_ref[...] = st * jnp.exp(b_last) + _dot_tn(v_bf, k_dec)

        o = o * lax.rsqrt(jnp.mean(o * o, axis=-1, keepdims=True) + NORM_EPS) * gn_ref[...]
        r = r_ref[rows, :]
        o_ref[rows, :] = (o * (r / (1.0 + jnp.exp(-r)))).astype(o_ref.dtype)


def _gla(proj, log_a, gn, bsz, seq, q_off, k_off, v_off, r_off, dk, dv, tt):
    nt = seq // tt
    tab = _gla_tables()

    def spec(width, off):
        return pl.BlockSpec((tt, width), lambda b, h, i: (b * nt + i, off // width + h))

    return pl.pallas_call(
        functools.partial(_gla_body, dk_scale=dk ** -0.5),
        grid=(bsz, GLA_HEADS, nt),
        in_specs=[spec(dk, q_off), spec(dk, k_off), spec(dk, 0), spec(dv, v_off), spec(dv, r_off),
                  pl.BlockSpec((None, 1, dv), lambda b, h, i: (h, 0, 0)),
                  pl.BlockSpec(tab.shape, lambda b, h, i: (0, 0))],
        out_specs=pl.BlockSpec((tt, dv), lambda b, h, i: (b * nt + i, h)),
        out_shape=jax.ShapeDtypeStruct((bsz * seq, GLA_HEADS * dv), BF16),
        scratch_shapes=[pltpu.VMEM((dv, dk), F32)],
        compiler_params=_params("parallel", "parallel", "arbitrary"),
        name="gla",
    )(proj, proj, log_a, proj, proj, gn.reshape(GLA_HEADS, 1, dv), tab)


def _proj_ln_body(a0_ref, a1_ref, w0_ref, w1_ref, x_ref, g_ref, b_ref, o_ref, *, alpha):
    y = _dot(a0_ref[...], w0_ref[...]) + _dot(a1_ref[...], w1_ref[...])
    o_ref[...] = _layer_norm_rows(alpha * x_ref[...] + y, g_ref[...], b_ref[...])


def _proj_ln(a0, a1, w0, w1, x, g, b, alpha, tm):
    m, d = x.shape
    row = lambda i: (i, 0)
    fixed = lambda i: (0, 0)
    return pl.pallas_call(
        functools.partial(_proj_ln_body, alpha=alpha),
        grid=(m // tm,),
        in_specs=[pl.BlockSpec((tm, a0.shape[1]), row), pl.BlockSpec((tm, a1.shape[1]), row),
                  pl.BlockSpec(w0.shape, fixed), pl.BlockSpec(w1.shape, fixed),
                  pl.BlockSpec((tm, d), row), pl.BlockSpec((1, d), fixed), pl.BlockSpec((1, d), fixed)],
        out_specs=pl.BlockSpec((tm, d), row),
        out_shape=jax.ShapeDtypeStruct((m, d), F32),
        compiler_params=_params("parallel"),
        name="mixer_out_ln",
    )(a0, a1, w0, w1, x, g.reshape(1, d), b.reshape(1, d))


def _xattn_body(x_ref, kv_ref, wq_ref, wo_ref, g_ref, b_ref, o_ref, *, alpha):
    x = x_ref[...]
    xa = wq_ref.shape[1]
    hd = xa // XA_HEADS
    q = _dot(x.astype(BF16), wq_ref[...])
    heads = []
    for h in range(XA_HEADS):
        qh = q[:, h * hd:(h + 1) * hd].astype(BF16)
        kh = kv_ref[:, h * hd:(h + 1) * hd].astype(BF16)
        vh = kv_ref[:, xa + h * hd:xa + (h + 1) * hd].astype(BF16)
        s = _dot_nt(qh, kh) * hd ** -0.5
        p = jnp.exp(s - jnp.max(s, axis=-1, keepdims=True))
        p = p / jnp.sum(p, axis=-1, keepdims=True)
        heads.append(_dot(p.astype(BF16), vh))
    o = jnp.concatenate(heads, axis=1).astype(BF16)
    y = _dot(o, wo_ref[...])
    o_ref[...] = _layer_norm_rows(alpha * x + y, g_ref[...], b_ref[...])


def _cross_attn(x, kv, wq, wo, g, b, alpha, bsz, seq, tm):
    m, d = x.shape
    nt = seq // tm
    mem_len = kv.shape[0] // bsz
    fixed = lambda bb, i: (0, 0)
    return pl.pallas_call(
        functools.partial(_xattn_body, alpha=alpha),
        grid=(bsz, nt),
        in_specs=[pl.BlockSpec((tm, d), lambda bb, i: (bb * nt + i, 0)),
                  pl.BlockSpec((mem_len, kv.shape[1]), lambda bb, i: (bb, 0)),
                  pl.BlockSpec(wq.shape, fixed), pl.BlockSpec(wo.shape, fixed),
                  pl.BlockSpec((1, d), fixed), pl.BlockSpec((1, d), fixed)],
        out_specs=pl.BlockSpec((tm, d), lambda bb, i: (bb * nt + i, 0)),
        out_shape=jax.ShapeDtypeStruct((m, d), F32),
        compiler_params=_params("parallel", "parallel"),
        name="cross_attn_ln",
    )(x, kv, wq, wo, g.reshape(1, d), b.reshape(1, d))


def _router_body(x_ref, wt_ref, b_ref, idx_ref, gate_ref):
    logits = _dot_nt(wt_ref[...], x_ref[...].astype(BF16)) + b_ref[...]
    n_e = logits.shape[0]
    e_iota = lax.broadcasted_iota(jnp.int32, logits.shape, 0)
    vals, idxs = [], []
    for _ in range(TOP_K):
        best = jnp.max(logits, axis=0, keepdims=True)
        sel = jnp.min(jnp.where(logits == best, e_iota, n_e), axis=0, keepdims=True)
        vals.append(best)
        idxs.append(sel)
        logits = jnp.where(e_iota == sel, -jnp.inf, logits)
    exps = [jnp.exp(v - vals[0]) for v in vals]
    total = exps[0] + exps[1] + exps[2] + exps[3]
    idx_ref[...] = jnp.concatenate(idxs, axis=0)
    gate_ref[...] = jnp.concatenate([e / total for e in exps], axis=0)


def _router(x, wt, b, tm):
    m, d = x.shape
    n_e = wt.shape[0]
    return pl.pallas_call(
        _router_body,
        grid=(m // tm,),
        in_specs=[pl.BlockSpec((tm, d), lambda i: (i, 0)),
                  pl.BlockSpec((n_e, d), lambda i: (0, 0)),
                  pl.BlockSpec((n_e, 1), lambda i: (0, 0))],
        out_specs=[pl.BlockSpec((TOP_K, tm), lambda i: (0, i)),
                   pl.BlockSpec((TOP_K, tm), lambda i: (0, i))],
        out_shape=[jax.ShapeDtypeStruct((TOP_K, m), jnp.int32),
                   jax.ShapeDtypeStruct((TOP_K, m), F32)],
        compiler_params=_params("parallel"),
        name="router_topk",
    )(x, wt, b.reshape(n_e, 1))


def _dispatch_tables(idx_t, n_experts):
    n_tok = idx_t.shape[1]
    n_asg = n_tok * TOP_K
    n_slots = n_asg + n_experts * MOE_SUB
    n_pass_max = n_asg // MOE_PASS + n_experts
    flat_e = idx_t.T.reshape(n_asg)
    onehot = (flat_e[:, None] == jnp.arange(n_experts, dtype=jnp.int32)[None, :]).astype(jnp.int32)
    csum = jnp.cumsum(onehot, axis=0)
    counts = csum[-1]
    rank = jnp.sum(csum * onehot, axis=1) - 1
    padded = (counts + MOE_SUB - 1) // MOE_SUB * MOE_SUB
    row_end = jnp.cumsum(padded)
    row_start = row_end - padded
    pos = (row_start[flat_e] + rank).astype(jnp.int32)
    asg = jnp.arange(n_asg, dtype=jnp.int32)
    slot_tok = jnp.zeros((n_slots,), jnp.int32).at[pos].set(asg // TOP_K)
    n_pass = (padded + MOE_PASS - 1) // MOE_PASS
    pass_end = jnp.cumsum(n_pass)
    pass_start = pass_end - n_pass
    total = pass_end[-1]
    i = jnp.arange(n_pass_max, dtype=jnp.int32)
    i_eff = jnp.minimum(i, total - 1)
    p_e = jnp.clip(jnp.searchsorted(pass_end, i_eff, side="right"), 0, n_experts - 1).astype(jnp.int32)
    p_off = (i_eff - pass_start[p_e]) * MOE_PASS
    p_row = (row_start[p_e] + p_off).astype(jnp.int32)
    p_chunks = jnp.clip((padded[p_e] - p_off) // MOE_SUB, 0, MOE_PASS // MOE_SUB)
    p_chunks = jnp.where(i < total, p_chunks, 0).astype(jnp.int32)
    used_rows = row_end[-1].astype(jnp.int32).reshape(1)
    return pos, slot_tok, used_rows, p_e, p_row, p_chunks, n_slots


def _gather_body(used_ref, tok_ref, x_hbm, o_ref, buf_ref, sem):
    j = pl.program_id(0)

    def row_copy(r):
        return pltpu.make_async_copy(x_hbm.at[pl.ds(tok_ref[0, r], 1)], buf_ref.at[pl.ds(r, 1)], sem)

    @pl.when(j * MOE_SUB < used_ref[0])
    def _():
        def start(r, carry):
            row_copy(r).start()
            return carry

        lax.fori_loop(0, MOE_SUB, start, 0, unroll=8)
        pltpu.make_async_copy(buf_ref, buf_ref, sem).wait()
        o_ref[...] = buf_ref[...].astype(o_ref.dtype)

    @pl.when(j * MOE_SUB >= used_ref[0])
    def _():
        o_ref[...] = jnp.zeros_like(o_ref)


def _dispatch_gather(x, slot_tok, used_rows, n_slots):
    d = x.shape[1]
    n_blk = n_slots // MOE_SUB
    return pl.pallas_call(
        _gather_body,
        grid_spec=pltpu.PrefetchScalarGridSpec(
            num_scalar_prefetch=1,
            grid=(n_blk,),
            in_specs=[pl.BlockSpec((None, 1, MOE_SUB), lambda j, used: (j, 0, 0),
                                   memory_space=pltpu.SMEM),
                      pl.BlockSpec(memory_space=pl.ANY)],
            out_specs=pl.BlockSpec((MOE_SUB, d), lambda j, used: (j, 0)),
            scratch_shapes=[pltpu.VMEM((MOE_SUB, d), F32), pltpu.SemaphoreType.DMA(())],
        ),
        out_shape=jax.ShapeDtypeStruct((n_slots, d), BF16),
        compiler_params=_params("arbitrary"),
        name="moe_gather",
    )(used_rows, slot_tok.reshape(n_blk, 1, MOE_SUB), x)


def _expert_body(e_ref, row_ref, nch_ref, used_ref, x_hbm, wgu_ref, wd_ref, bgu_ref, bd_ref, y_hbm,
                 x_buf, acc, gu_buf, wgu_bf, wd_perm, wd_bf, x_sem, y_sem):
    i, j = pl.program_id(0), pl.program_id(1)
    n_items, last = pl.num_programs(0), pl.num_programs(1) - 1
    nch = nch_ref[i]
    slot = i % 2
    tf = wd_ref.shape[0]
    half = tf // 2

    def chunk(c):
        return pl.ds(pl.multiple_of(c * MOE_SUB, MOE_SUB), MOE_SUB)

    def x_copy(item, c):
        rows = pl.ds(pl.multiple_of(row_ref[item] + c * MOE_SUB, MOE_SUB), MOE_SUB)
        return pltpu.make_async_copy(x_hbm.at[rows], x_buf.at[item % 2, chunk(c)], x_sem.at[item % 2])

    def y_copy(item, c):
        rows = pl.ds(pl.multiple_of(row_ref[item] + c * MOE_SUB, MOE_SUB), MOE_SUB)
        return pltpu.make_async_copy(acc.at[chunk(c)], y_hbm.at[rows], y_sem)

    def for_chunks(item, fn):
        def step(c, carry):
            fn(item, c)
            return carry

        lax.fori_loop(0, nch_ref[item], step, 0)

    @pl.when(j == 0)
    def _():
        @pl.when(i == 0)
        def _():
            for_chunks(0, lambda it, c: x_copy(it, c).start())

        @pl.when(i + 1 < n_items)
        def _():
            for_chunks(i + 1, lambda it, c: x_copy(it, c).start())

        @pl.when(i > 0)
        def _():
            for_chunks(i - 1, lambda it, c: y_copy(it, c).wait())

        for_chunks(i, lambda it, c: x_copy(it, c).wait())
        acc[...] = jnp.broadcast_to(bd_ref[...], acc.shape)

    @pl.when(nch > 0)
    def _():
        wgu_bf[...] = wgu_ref[...].astype(BF16)
        for cb in range(wd_ref.shape[1] // LANE):
            cols = pl.ds(cb * LANE, LANE)
            wd_perm[cb, pl.ds(0, half, stride=2), :] = wd_ref[0:half, cols]
            wd_perm[cb, pl.ds(1, half, stride=2), :] = wd_ref[half:tf, cols]
            wd_bf[:, cols] = wd_perm[cb].astype(BF16)

        even_lane = lax.broadcasted_iota(jnp.int32, (1, tf), 1) % 2 == 0

        def up_proj(c):
            return _dot(x_buf[slot, chunk(c), :], wgu_bf[...]) + bgu_ref[...]

        def down_proj(c, gu):
            up = jnp.clip(pltpu.roll(gu, 2 * tf - 1, 1), -SWIGLU_LIMIT, SWIGLU_LIMIT)
            gate = jnp.minimum(gu, SWIGLU_LIMIT)
            act = (up + 1.0) * (gate / (1.0 + jnp.exp(-SWIGLU_ALPHA * gate)))
            packed = jnp.where(even_lane, act[:, :tf], pltpu.roll(act[:, tf:], 1, 1))
            acc[chunk(c), :] += _dot(packed.astype(BF16), wd_bf[...])

        gu_buf[0] = up_proj(0)

        def step(c, carry):
            gu = gu_buf[c % 2]
            gu_next = up_proj(c + 1)
            down_proj(c, gu)
            gu_buf[(c + 1) % 2] = gu_next
            return carry

        lax.fori_loop(0, nch - 1, step, 0)
        down_proj(nch - 1, gu_buf[(nch - 1) % 2])

    @pl.when(j == last)
    def _():
        for_chunks(i, lambda it, c: y_copy(it, c).start())

        @pl.when(i == n_items - 1)
        def _():
            for_chunks(i, lambda it, c: y_copy(it, c).wait())
            acc[chunk(0), :] = jnp.zeros((MOE_SUB, acc.shape[1]), acc.dtype)

            def fill_copy(c):
                return pltpu.make_async_copy(acc.at[chunk(0)], y_hbm.at[chunk(c)], y_sem)

            def fill_start(c, carry):
                fill_copy(c).start()
                return carry

            def fill_wait(c, carry):
                fill_copy(c).wait()
                return carry

            first, end = used_ref[0] // MOE_SUB, y_hbm.shape[0] // MOE_SUB
            lax.fori_loop(first, end, fill_start, 0)
            lax.fori_loop(first, end, fill_wait, 0)


def _experts(xb, w_gu, w_down, b_gu, b_down, layer, p_e, p_row, p_chunks, used_rows, tf):
    n_slots, d = xb.shape
    n_l, n_e, f, _ = w_down.shape
    nf = f // tf
    n_items = p_e.shape[0]

    def jeff(i, j, nch):
        return jnp.where(nch[i] > 0, j, nf - 1)

    return pl.pallas_call(
        _expert_body,
        grid_spec=pltpu.PrefetchScalarGridSpec(
            num_scalar_prefetch=4,
            grid=(n_items, nf),
            in_specs=[
                pl.BlockSpec(memory_space=pl.ANY),
                pl.BlockSpec((None, None, d, 2 * tf), lambda i, j, e, row, nch, used: (layer, e[i], 0, jeff(i, j, nch))),
                pl.BlockSpec((None, None, tf, d), lambda i, j, e, row, nch, used: (layer, e[i], jeff(i, j, nch), 0)),
                pl.BlockSpec((None, None, 1, 2 * tf), lambda i, j, e, row, nch, used: (layer, e[i], 0, jeff(i, j, nch))),
                pl.BlockSpec((None, None, 1, d), lambda i, j, e, row, nch, used: (layer, e[i], 0, 0)),
            ],
            out_specs=pl.BlockSpec(memory_space=pl.ANY),
            scratch_shapes=[pltpu.VMEM((2, MOE_PASS, d), BF16), pltpu.VMEM((MOE_PASS, d), F32),
                            pltpu.VMEM((2, MOE_SUB, 2 * tf), F32), pltpu.VMEM((d, 2 * tf), BF16),
                            pltpu.VMEM((d // LANE, tf, LANE), F32), pltpu.VMEM((tf, d), BF16),
                            pltpu.SemaphoreType.DMA((2,)), pltpu.SemaphoreType.DMA(())],
        ),
        out_shape=jax.ShapeDtypeStruct((n_slots, d), F32),
        compiler_params=_params("arbitrary", "arbitrary"),
        name="moe_experts",
    )(p_e, p_row, p_chunks, used_rows, xb, w_gu, w_down, b_gu.reshape(n_l, n_e, 1, 2 * f),
      b_down.reshape(n_l, n_e, 1, d))


def _combine_body(pos_ref, x_ref, gate_ref, y_hbm, g_ref, b_ref, o_ref, buf_ref, sem, *, alpha):
    tm = x_ref.shape[0]

    def row_copy(a):
        t, k = a // TOP_K, a % TOP_K
        return pltpu.make_async_copy(y_hbm.at[pl.ds(pos_ref[0, a], 1)], buf_ref.at[k, pl.ds(t, 1)], sem)

    def start(a, carry):
        row_copy(a).start()
        return carry

    lax.fori_loop(0, tm * TOP_K, start, 0, unroll=8)
    pltpu.make_async_copy(buf_ref, buf_ref, sem).wait()
    gate = gate_ref[...]
    y = ((gate[:, 0:1] * buf_ref[0] + gate[:, 1:2] * buf_ref[1])
         + (gate[:, 2:3] * buf_ref[2] + gate[:, 3:4] * buf_ref[3]))
    o_ref[...] = _layer_norm_rows(alpha * x_ref[...] + y, g_ref[...], b_ref[...])


def _combine_ln(x, yb, pos, gate, g, b, alpha, tm):
    m, d = x.shape
    nt = m // tm
    return pl.pallas_call(
        functools.partial(_combine_body, alpha=alpha),
        grid=(nt,),
        in_specs=[pl.BlockSpec((None, 1, tm * TOP_K), lambda i: (i, 0, 0), memory_space=pltpu.SMEM),
                  pl.BlockSpec((tm, d), lambda i: (i, 0)),
                  pl.BlockSpec((tm, TOP_K), lambda i: (i, 0)),
                  pl.BlockSpec(memory_space=pl.ANY),
                  pl.BlockSpec((1, d), lambda i: (0, 0)), pl.BlockSpec((1, d), lambda i: (0, 0))],
        out_specs=pl.BlockSpec((tm, d), lambda i: (i, 0)),
        out_shape=jax.ShapeDtypeStruct((m, d), F32),
        scratch_shapes=[pltpu.VMEM((TOP_K, tm, d), F32), pltpu.SemaphoreType.DMA(())],
        compiler_params=_params("arbitrary"),
        name="moe_combine_ln",
    )(pos.reshape(nt, 1, tm * TOP_K), x, gate, yb, g.reshape(1, d), b.reshape(1, d))


def _mixer(x, w_in, conv_w, w_gate2, b_gate2, gla_norm_g, w_out, ln_g, ln_b, alpha, bsz, seq):
    d = x.shape[1]
    ch = conv_w.shape[1]
    dk_total = w_gate2.shape[1]
    dk = dk_total // GLA_HEADS
    dv = gla_norm_g.shape[1]
    main_cols = w_in.shape[1] - GLA_GATE_RANK
    proj = _matmul(x, w_in[:, :main_cols].astype(BF16), tm=1024, tn=512, name="mixer_in_proj")
    w1p = jnp.zeros((d, LANE), BF16).at[:, :GLA_GATE_RANK].set(w_in[:, main_cols:].astype(BF16))
    w2p = jnp.zeros((LANE, dk_total), BF16).at[:GLA_GATE_RANK, :].set(w_gate2.astype(BF16))
    log_a = _gla_gate(x, w1p, w2p, b_gate2.reshape(1, dk_total), tm=1024)
    y_conv = _short_conv(proj, conv_w, bsz, seq, ch, tt=512, cw=512)
    q_off = 3 * ch
    k_off = q_off + dk_total
    v_off = k_off + dk_total
    r_off = v_off + GLA_HEADS * dv
    y_gla = _gla(proj, log_a, gla_norm_g, bsz, seq, q_off, k_off, v_off, r_off, dk, dv, tt=512)
    w_out_bf = w_out.astype(BF16)
    return _proj_ln(y_conv, y_gla, w_out_bf[:ch], w_out_bf[ch:], x, ln_g, ln_b, alpha, tm=512)


def _moe(x, layer, w_router, b_router, w_gu, b_gu, w_down, b_down, ln_g, ln_b, alpha):
    n_e = w_router.shape[1]
    idx_t, gate_t = _router(x, w_router.T.astype(BF16), b_router, tm=1024)
    pos, slot_tok, used_rows, p_e, p_row, p_chunks, n_slots = _dispatch_tables(idx_t, n_e)
    xb = _dispatch_gather(x, slot_tok, used_rows, n_slots)
    yb = _experts(xb, w_gu, w_down, b_gu, b_down, layer, p_e, p_row, p_chunks, used_rows, tf=256)
    return _combine_ln(x, yb, pos, gate_t.T, ln_g, ln_b, alpha, tm=128)


def kernel(x, mem, w_in, conv_w, w_gate2, b_gate2, gla_norm_g, w_out, ln_mix_g, ln_mix_b, w_xq, w_xkv, w_xo, ln_xa_g, ln_xa_b, w_router, b_router, w_gu, b_gu, w_down, b_down, ln_moe_g, ln_moe_b):
    bsz, seq, d = x.shape
    depth = w_in.shape[0]
    alpha = (2 * depth) ** 0.25
    xf = x.reshape(bsz * seq, d)
    memf = mem.reshape(bsz * mem.shape[1], d)
    for l in range(depth):
        xf = _mixer(xf, w_in[l], conv_w[l], w_gate2[l], b_gate2[l], gla_norm_g[l], w_out[l],
                    ln_mix_g[l], ln_mix_b[l], alpha, bsz, seq)
        kv = _matmul(memf, w_xkv[l].astype(BF16), tm=memf.shape[0], tn=512, name="xattn_kv_proj")
        xf = _cross_attn(xf, kv, w_xq[l].astype(BF16), w_xo[l].astype(BF16), ln_xa_g[l], ln_xa_b[l],
                         alpha, bsz, seq, tm=512)
        xf = _moe(xf, l, w_router[l], b_router[l], w_gu, b_gu, w_down, b_down,
                  ln_moe_g[l], ln_moe_b[l], alpha)
    return xf.reshape(bsz, seq, d)
```

```python
import functools

import numpy as np
import jax
import jax.numpy as jnp
from jax import lax
from jax.experimental import pallas as pl
from jax.experimental.pallas import tpu as pltpu

CONV_W = 3
GLA_HEADS = 4
GLA_GATE_RANK = 16
GLA_TAU = 16.0
XA_HEADS = 4
TOP_K = 4
SWIGLU_LIMIT = 7.0
SWIGLU_ALPHA = 1.702
LN_EPS = 1e-5
NORM_EPS = 1e-6

LANE = 128
GLA_CHUNK = 128
GLA_LEVELS = 7
MOE_SUB = 256
MOE_PASS = 1536
ROW_TILES = 16
ROW_PITCH = 24
VMEM_LIMIT = 56 * 1024 * 1024

BF16 = jnp.bfloat16
F32 = jnp.float32


def _dot(a, b):
    return jnp.dot(a, b, preferred_element_type=F32)


def _dot_nt(a, b):
    return lax.dot_general(a, b, (((1,), (1,)), ((), ())), preferred_element_type=F32)


def _dot_tn(a, b):
    return lax.dot_general(a, b, (((0,), (0,)), ((), ())), preferred_element_type=F32)


def _params(*sem):
    return pltpu.CompilerParams(dimension_semantics=sem, vmem_limit_bytes=VMEM_LIMIT)


def _store_row_major(dst_ref, val):
    n = val.shape[0]
    for c in range(ROW_TILES):
        dst_ref[pl.ds(c, n, stride=ROW_TILES), :] = val[:, c * LANE:(c + 1) * LANE]


def _load_row_major(src_ref, first, n, pitch):
    return [src_ref[pl.ds(first + c, n, stride=pitch), :] for c in range(ROW_TILES)]


def _layer_norm_rows(y, g, b):
    mu = jnp.mean(y, axis=-1, keepdims=True)
    yc = y - mu
    var = jnp.mean(yc * yc, axis=-1, keepdims=True)
    return yc * lax.rsqrt(var + LN_EPS) * g + b


def _matmul_body(a_ref, w_ref, o_ref, abf_ref):
    @pl.when(pl.program_id(1) == 0)
    def _():
        abf_ref[...] = a_ref[...].astype(BF16)

    o_ref[...] = _dot(abf_ref[...], w_ref[...]).astype(o_ref.dtype)


def _matmul(a, w, tm, tn, name):
    m, k = a.shape
    n = w.shape[1]
    return pl.pallas_call(
        _matmul_body,
        grid=(m // tm, n // tn),
        in_specs=[pl.BlockSpec((tm, k), lambda i, j: (i, 0)),
                  pl.BlockSpec((k, tn), lambda i, j: (0, j))],
        out_specs=pl.BlockSpec((tm, tn), lambda i, j: (i, j)),
        out_shape=jax.ShapeDtypeStruct((m, n), F32),
        scratch_shapes=[pltpu.VMEM((tm, k), BF16)],
        compiler_params=_params("parallel", "arbitrary"),
        name=name,
    )(a, w)


def _gate_body(x_ref, w1_ref, w2_ref, b_ref, o_ref):
    low = _dot(x_ref[...].astype(BF16), w1_ref[...])
    z = _dot(low.astype(BF16), w2_ref[...]) + b_ref[...]
    log_sig = jnp.minimum(z, 0.0) - jnp.log1p(jnp.exp(-jnp.abs(z)))
    o_ref[...] = log_sig / GLA_TAU


def _gla_gate(x, w1p, w2p, b2, tm):
    m, d = x.shape
    n = w2p.shape[1]
    return pl.pallas_call(
        _gate_body,
        grid=(m // tm,),
        in_specs=[pl.BlockSpec((tm, d), lambda i: (i, 0)),
                  pl.BlockSpec(w1p.shape, lambda i: (0, 0)),
                  pl.BlockSpec(w2p.shape, lambda i: (0, 0)),
                  pl.BlockSpec((1, n), lambda i: (0, 0))],
        out_specs=pl.BlockSpec((tm, n), lambda i: (i, 0)),
        out_shape=jax.ShapeDtypeStruct((m, n), F32),
        compiler_params=_params("parallel"),
        name="gla_gate",
    )(x, w1p, w2p, b2)


def _conv_body(gc_ref, gb_ref, h_ref, gch_ref, hh_ref, w_ref, o_ref):
    u = gc_ref[...] * h_ref[...]
    uh = jnp.where(pl.program_id(1) == 0, 0.0, gch_ref[...] * hh_ref[...])
    row = lax.broadcasted_iota(jnp.int32, u.shape, 0)
    u1 = jnp.where(row == 0, uh[7:8, :], pltpu.roll(u, 1, 0))
    u2 = pltpu.roll(u, 2, 0)
    u2 = jnp.where(row == 0, uh[6:7, :], jnp.where(row == 1, uh[7:8, :], u2))
    w = w_ref[...]
    conv = w[0:1, :] * u2 + w[1:2, :] * u1 + w[2:3, :] * u
    o_ref[...] = (gb_ref[...] * conv).astype(o_ref.dtype)


def _short_conv(proj, conv_w, bsz, seq, ch, tt, cw):
    nt, nc = seq // tt, ch // cw

    def main(off):
        return pl.BlockSpec((tt, cw), lambda b, i, c: (b * nt + i, off * nc + c))

    def halo(off):
        return pl.BlockSpec(
            (8, cw), lambda b, i, c: (jnp.maximum((b * seq + i * tt) // 8 - 1, 0), off * nc + c))

    return pl.pallas_call(
        _conv_body,
        grid=(bsz, nt, nc),
        in_specs=[main(0), main(1), main(2), halo(0), halo(2),
                  pl.BlockSpec((CONV_W, cw), lambda b, i, c: (0, c))],
        out_specs=pl.BlockSpec((tt, cw), lambda b, i, c: (b * nt + i, c)),
        out_shape=jax.ShapeDtypeStruct((bsz * seq, ch), BF16),
        compiler_params=_params("parallel", "parallel", "parallel"),
        name="short_conv",
    )(proj, proj, proj, proj, proj, conv_w)


def _gla_tables():
    c = GLA_CHUNK
    i = np.arange(c)[:, None]
    t = np.arange(c)[None, :]
    mats = []
    for lvl in range(GLA_LEVELS):
        m = ((i >> (lvl + 1)) << (lvl + 1)) + (1 << lvl) - 1
        upper = (i > m) & (t > m) & (t <= i)
        lower = (i <= m) & (t > i) & (t <= m)
        mats.append(upper | lower)
    mats.append(t <= i)
    return jnp.asarray(np.concatenate(mats, axis=0).astype(np.float32), dtype=BF16)


def _gla_body(q_ref, k_ref, g_ref, v_ref, r_ref, gn_ref, tab_ref, o_ref, st_ref, *, dk_scale):
    c = GLA_CHUNK
    dk = q_ref.shape[1]

    @pl.when(pl.program_id(2) == 0)
    def _():
        st_ref[...] = jnp.zeros_like(st_ref)

    ii = lax.broadcasted_iota(jnp.int32, (c, c), 0)
    jj = lax.broadcasted_iota(jnp.int32, (c, c), 1)
    xor = ii ^ jj
    lower = ii > jj

    for ch in range(q_ref.shape[0] // c):
        rows = pl.ds(ch * c, c)
        q = q_ref[rows, :] * dk_scale
        k = k_ref[rows, :]
        g = g_ref[rows, :]
        v_bf = v_ref[rows, :].astype(BF16)
        g_hi = g.astype(BF16)
        g_lo = (g - g_hi.astype(F32)).astype(BF16)
        sums = _dot(tab_ref[...], jnp.concatenate([g_hi, g_lo], axis=1))
        sums = sums[:, :dk] + sums[:, dk:]

        scores = jnp.where(ii == jj, _dot_nt(q.astype(BF16), k.astype(BF16)), 0.0)
        for lvl in range(GLA_LEVELS):
            w = jnp.exp(sums[lvl * c:(lvl + 1) * c, :])
            s_l = _dot_nt((q * w).astype(BF16), (k * w).astype(BF16))
            scores = jnp.where(((xor >> lvl) == 1) & lower, s_l, scores)

        b = sums[GLA_LEVELS * c:, :]
        b_last = b[c - 1:c, :]
        st = st_ref[...]
        o = _dot_nt((q * jnp.exp(b)).astype(BF16), st.astype(BF16)) + _dot(scores.astype(BF16), v_bf)
        k_dec = (k * jnp.exp(b_last - b)).astype(BF16)
        st_ref[...] = st * jnp.exp(b_last) + _dot_tn(v_bf, k_dec)

        o = o * lax.rsqrt(jnp.mean(o * o, axis=-1, keepdims=True) + NORM_EPS) * gn_ref[...]
        r = r_ref[rows, :]
        o_ref[rows, :] = (o * (r / (1.0 + jnp.exp(-r)))).astype(o_ref.dtype)


def _gla(proj, log_a, gn, bsz, seq, q_off, k_off, v_off, r_off, dk, dv, tt):
    nt = seq // tt
    tab = _gla_tables()

    def spec(width, off):
        return pl.BlockSpec((tt, width), lambda b, h, i: (b * nt + i, off // width + h))

    return pl.pallas_call(
        functools.partial(_gla_body, dk_scale=dk ** -0.5),
        grid=(bsz, GLA_HEADS, nt),
        in_specs=[spec(dk, q_off), spec(dk, k_off), spec(dk, 0), spec(dv, v_off), spec(dv, r_off),
                  pl.BlockSpec((None, 1, dv), lambda b, h, i: (h, 0, 0)),
                  pl.BlockSpec(tab.shape, lambda b, h, i: (0, 0))],
        out_specs=pl.BlockSpec((tt, dv), lambda b, h, i: (b * nt + i, h)),
        out_shape=jax.ShapeDtypeStruct((bsz * seq, GLA_HEADS * dv), BF16),
        scratch_shapes=[pltpu.VMEM((dv, dk), F32)],
        compiler_params=_params("parallel", "parallel", "arbitrary"),
        name="gla",
    )(proj, proj, log_a, proj, proj, gn.reshape(GLA_HEADS, 1, dv), tab)


def _proj_ln_body(a0_ref, a1_ref, w0_ref, w1_ref, x_ref, g_ref, b_ref, o_ref, *, alpha):
    y = _dot(a0_ref[...], w0_ref[...]) + _dot(a1_ref[...], w1_ref[...])
    o_ref[...] = _layer_norm_rows(alpha * x_ref[...] + y, g_ref[...], b_ref[...])


def _proj_ln(a0, a1, w0, w1, x, g, b, alpha, tm):
    m, d = x.shape
    row = lambda i: (i, 0)
    fixed = lambda i: (0, 0)
    return pl.pallas_call(
        functools.partial(_proj_ln_body, alpha=alpha),
        grid=(m // tm,),
        in_specs=[pl.BlockSpec((tm, a0.shape[1]), row), pl.BlockSpec((tm, a1.shape[1]), row),
                  pl.BlockSpec(w0.shape, fixed), pl.BlockSpec(w1.shape, fixed),
                  pl.BlockSpec((tm, d), row), pl.BlockSpec((1, d), fixed), pl.BlockSpec((1, d), fixed)],
        out_specs=pl.BlockSpec((tm, d), row),
        out_shape=jax.ShapeDtypeStruct((m, d), F32),
        compiler_params=_params("parallel"),
        name="mixer_out_ln",
    )(a0, a1, w0, w1, x, g.reshape(1, d), b.reshape(1, d))


def _xattn_body(x_ref, kv_ref, wq_ref, wo_ref, g_ref, b_ref, o_ref, rows_ref, *, alpha):
    x = x_ref[...]
    xa = wq_ref.shape[1]
    hd = xa // XA_HEADS
    q = _dot(x.astype(BF16), wq_ref[...])
    heads = []
    for h in range(XA_HEADS):
        qh = q[:, h * hd:(h + 1) * hd].astype(BF16)
        kh = kv_ref[:, h * hd:(h + 1) * hd].astype(BF16)
        vh = kv_ref[:, xa + h * hd:xa + (h + 1) * hd].astype(BF16)
        s = _dot_nt(qh, kh) * hd ** -0.5
        p = jnp.exp(s - jnp.max(s, axis=-1, keepdims=True))
        p = p / jnp.sum(p, axis=-1, keepdims=True)
        heads.append(_dot(p.astype(BF16), vh))
    o = jnp.concatenate(heads, axis=1).astype(BF16)
    y = _dot(o, wo_ref[...])
    out = _layer_norm_rows(alpha * x + y, g_ref[...], b_ref[...])
    o_ref[...] = out
    _store_row_major(rows_ref, out)


def _cross_attn(x, kv, wq, wo, g, b, alpha, bsz, seq, tm):
    m, d = x.shape
    nt = seq // tm
    mem_len = kv.shape[0] // bsz
    fixed = lambda bb, i: (0, 0)
    return pl.pallas_call(
        functools.partial(_xattn_body, alpha=alpha),
        grid=(bsz, nt),
        in_specs=[pl.BlockSpec((tm, d), lambda bb, i: (bb * nt + i, 0)),
                  pl.BlockSpec((mem_len, kv.shape[1]), lambda bb, i: (bb, 0)),
                  pl.BlockSpec(wq.shape, fixed), pl.BlockSpec(wo.shape, fixed),
                  pl.BlockSpec((1, d), fixed), pl.BlockSpec((1, d), fixed)],
        out_specs=[pl.BlockSpec((tm, d), lambda bb, i: (bb * nt + i, 0)),
                   pl.BlockSpec((tm * ROW_TILES, LANE), lambda bb, i: (bb * nt + i, 0))],
        out_shape=[jax.ShapeDtypeStruct((m, d), F32),
                   jax.ShapeDtypeStruct((m * ROW_TILES, LANE), F32)],
        compiler_params=_params("parallel", "parallel"),
        name="cross_attn_ln",
    )(x, kv, wq, wo, g.reshape(1, d), b.reshape(1, d))


def _router_body(x_ref, wt_ref, b_ref, idx_ref, gate_ref):
    logits = _dot_nt(wt_ref[...], x_ref[...].astype(BF16)) + b_ref[...]
    n_e = logits.shape[0]
    e_iota = lax.broadcasted_iota(jnp.int32, logits.shape, 0)
    vals, idxs = [], []
    for _ in range(TOP_K):
        best = jnp.max(logits, axis=0, keepdims=True)
        sel = jnp.min(jnp.where(logits == best, e_iota, n_e), axis=0, keepdims=True)
        vals.append(best)
        idxs.append(sel)
        logits = jnp.where(e_iota == sel, -jnp.inf, logits)
    exps = [jnp.exp(v - vals[0]) for v in vals]
    total = exps[0] + exps[1] + exps[2] + exps[3]
    idx_ref[...] = jnp.concatenate(idxs, axis=0)
    gate_ref[...] = jnp.concatenate([e / total for e in exps], axis=0)


def _router(x, wt, b, tm):
    m, d = x.shape
    n_e = wt.shape[0]
    return pl.pallas_call(
        _router_body,
        grid=(m // tm,),
        in_specs=[pl.BlockSpec((tm, d), lambda i: (i, 0)),
                  pl.BlockSpec((n_e, d), lambda i: (0, 0)),
                  pl.BlockSpec((n_e, 1), lambda i: (0, 0))],
        out_specs=[pl.BlockSpec((TOP_K, tm), lambda i: (0, i)),
                   pl.BlockSpec((TOP_K, tm), lambda i: (0, i))],
        out_shape=[jax.ShapeDtypeStruct((TOP_K, m), jnp.int32),
                   jax.ShapeDtypeStruct((TOP_K, m), F32)],
        compiler_params=_params("parallel"),
        name="router_topk",
    )(x, wt, b.reshape(n_e, 1))


def _dispatch_tables(idx_t, n_experts):
    n_tok = idx_t.shape[1]
    n_asg = n_tok * TOP_K
    n_slots = n_asg + n_experts * MOE_SUB
    n_pass_max = n_asg // MOE_PASS + n_experts
    flat_e = idx_t.T.reshape(n_asg)
    onehot = (flat_e[:, None] == jnp.arange(n_experts, dtype=jnp.int32)[None, :]).astype(jnp.int32)
    csum = jnp.cumsum(onehot, axis=0)
    counts = csum[-1]
    rank = jnp.sum(csum * onehot, axis=1) - 1
    padded = (counts + MOE_SUB - 1) // MOE_SUB * MOE_SUB
    row_end = jnp.cumsum(padded)
    row_start = row_end - padded
    pos = (row_start[flat_e] + rank).astype(jnp.int32)
    asg = jnp.arange(n_asg, dtype=jnp.int32)
    slot_tok = jnp.zeros((n_slots,), jnp.int32).at[pos].set(asg // TOP_K)
    n_pass = (padded + MOE_PASS - 1) // MOE_PASS
    pass_end = jnp.cumsum(n_pass)
    pass_start = pass_end - n_pass
    total = pass_end[-1]
    i = jnp.arange(n_pass_max, dtype=jnp.int32)
    i_eff = jnp.minimum(i, total - 1)
    p_e = jnp.clip(jnp.searchsorted(pass_end, i_eff, side="right"), 0, n_experts - 1).astype(jnp.int32)
    p_off = (i_eff - pass_start[p_e]) * MOE_PASS
    p_row = (row_start[p_e] + p_off).astype(jnp.int32)
    p_chunks = jnp.clip((padded[p_e] - p_off) // MOE_SUB, 0, MOE_PASS // MOE_SUB)
    p_chunks = jnp.where(i < total, p_chunks, 0).astype(jnp.int32)
    used_rows = row_end[-1].astype(jnp.int32).reshape(1)
    return pos, slot_tok, used_rows, p_e, p_row, p_chunks, n_slots


def _gather_body(used_ref, tok_ref, x_hbm, o_ref, buf_ref, sem):
    j = pl.program_id(0)

    def row_copy(r):
        src = pl.ds(pl.multiple_of(tok_ref[0, r] * ROW_TILES, ROW_TILES), ROW_TILES)
        dst = pl.ds(pl.multiple_of(r * ROW_PITCH, 8), ROW_TILES)
        return pltpu.make_async_copy(x_hbm.at[src], buf_ref.at[dst], sem)

    @pl.when(j * MOE_SUB < used_ref[0])
    def _():
        def start(r, carry):
            row_copy(r).start()
            return carry

        lax.fori_loop(0, MOE_SUB, start, 0, unroll=8)
        all_rows = pl.ds(0, MOE_SUB * ROW_TILES)
        pltpu.make_async_copy(x_hbm.at[all_rows], buf_ref.at[all_rows], sem).wait()
        for c, part in enumerate(_load_row_major(buf_ref, 0, MOE_SUB, ROW_PITCH)):
            o_ref[:, c * LANE:(c + 1) * LANE] = part.astype(o_ref.dtype)

    @pl.when(j * MOE_SUB >= used_ref[0])
    def _():
        o_ref[...] = jnp.zeros_like(o_ref)


def _dispatch_gather(x_rows, slot_tok, used_rows, n_slots):
    d = ROW_TILES * LANE
    n_blk = n_slots // MOE_SUB
    return pl.pallas_call(
        _gather_body,
        grid_spec=pltpu.PrefetchScalarGridSpec(
            num_scalar_prefetch=1,
            grid=(n_blk,),
            in_specs=[pl.BlockSpec((None, 1, MOE_SUB), lambda j, used: (j, 0, 0),
                                   memory_space=pltpu.SMEM),
                      pl.BlockSpec(memory_space=pl.ANY)],
            out_specs=pl.BlockSpec((MOE_SUB, d), lambda j, used: (j, 0)),
            scratch_shapes=[pltpu.VMEM((MOE_SUB * ROW_PITCH, LANE), F32), pltpu.SemaphoreType.DMA(())],
        ),
        out_shape=jax.ShapeDtypeStruct((n_slots, d), BF16),
        compiler_params=_params("arbitrary"),
        name="moe_gather",
    )(used_rows, slot_tok.reshape(n_blk, 1, MOE_SUB), x_rows)


def _expert_body(e_ref, row_ref, nch_ref, used_ref, x_hbm, wgu_ref, wd_ref, bgu_ref, bd_ref, y_hbm,
                 x_buf, acc, gu_buf, wgu_bf, wd_perm, wd_bf, y_stage, x_sem, y_sem):
    i, j = pl.program_id(0), pl.program_id(1)
    n_items, last = pl.num_programs(0), pl.num_programs(1) - 1
    nch = nch_ref[i]
    slot = i % 2
    tf = wd_ref.shape[0]
    half = tf // 2

    def chunk(c):
        return pl.ds(pl.multiple_of(c * MOE_SUB, MOE_SUB), MOE_SUB)

    def x_copy(item, c):
        rows = pl.ds(pl.multiple_of(row_ref[item] + c * MOE_SUB, MOE_SUB), MOE_SUB)
        return pltpu.make_async_copy(x_hbm.at[rows], x_buf.at[item % 2, chunk(c)], x_sem.at[item % 2])

    def y_copy(first_row, s):
        rows = pl.ds(pl.multiple_of(first_row * ROW_TILES, MOE_SUB * ROW_TILES), MOE_SUB * ROW_TILES)
        return pltpu.make_async_copy(y_stage.at[s], y_hbm.at[rows], y_sem.at[s])

    def y_drain(n_chunks):
        for s in range(2):
            @pl.when(n_chunks > s)
            def _():
                y_copy(0, s).wait()

    def for_chunks(item, fn):
        def step(c, carry):
            fn(item, c)
            return carry

        lax.fori_loop(0, nch_ref[item], step, 0)

    @pl.when(j == 0)
    def _():
        @pl.when(i == 0)
        def _():
            for_chunks(0, lambda it, c: x_copy(it, c).start())

        @pl.when(i + 1 < n_items)
        def _():
            for_chunks(i + 1, lambda it, c: x_copy(it, c).start())

        for_chunks(i, lambda it, c: x_copy(it, c).wait())
        acc[...] = jnp.broadcast_to(bd_ref[...], acc.shape)

    @pl.when(nch > 0)
    def _():
        wgu_bf[...] = wgu_ref[...].astype(BF16)
        for cb in range(wd_ref.shape[1] // LANE):
            cols = pl.ds(cb * LANE, LANE)
            wd_perm[cb, pl.ds(0, half, stride=2), :] = wd_ref[0:half, cols]
            wd_perm[cb, pl.ds(1, half, stride=2), :] = wd_ref[half:tf, cols]
            wd_bf[:, cols] = wd_perm[cb].astype(BF16)

        even_lane = lax.broadcasted_iota(jnp.int32, (1, tf), 1) % 2 == 0

        def up_proj(c):
            return _dot(x_buf[slot, chunk(c), :], wgu_bf[...]) + bgu_ref[...]

        def down_proj(c, gu):
            up = jnp.clip(pltpu.roll(gu, 2 * tf - 1, 1), -SWIGLU_LIMIT, SWIGLU_LIMIT)
            gate = jnp.minimum(gu, SWIGLU_LIMIT)
            act = (up + 1.0) * (gate / (1.0 + jnp.exp(-SWIGLU_ALPHA * gate)))
            packed = jnp.where(even_lane, act[:, :tf], pltpu.roll(act[:, tf:], 1, 1))
            acc[chunk(c), :] += _dot(packed.astype(BF16), wd_bf[...])

        gu_buf[0] = up_proj(0)

        def step(c, carry):
            gu = gu_buf[c % 2]
            gu_next = up_proj(c + 1)
            down_proj(c, gu)
            gu_buf[(c + 1) % 2] = gu_next
            return carry

        lax.fori_loop(0, nch - 1, step, 0)
        down_proj(nch - 1, gu_buf[(nch - 1) % 2])

    @pl.when(j == last)
    def _():
        @pl.when(i > 0)
        def _():
            y_drain(nch_ref[i - 1])

        def emit(c, carry):
            s = c % 2

            @pl.when(c >= 2)
            def _():
                y_copy(0, s).wait()

            _store_row_major(y_stage.at[s], acc[chunk(c), :])
            y_copy(row_ref[i] + c * MOE_SUB, s).start()
            return carry

        lax.fori_loop(0, nch, emit, 0)

        @pl.when(i == n_items - 1)
        def _():
            y_drain(nch)
            y_stage[0] = jnp.zeros(y_stage.shape[1:], y_stage.dtype)

            def fill_start(c, carry):
                y_copy(c * MOE_SUB, 0).start()
                return carry

            def fill_wait(c, carry):
                y_copy(0, 0).wait()
                return carry

            first, end = used_ref[0] // MOE_SUB, y_hbm.shape[0] // (MOE_SUB * ROW_TILES)
            lax.fori_loop(first, end, fill_start, 0)
            lax.fori_loop(first, end, fill_wait, 0)


def _experts(xb, w_gu, w_down, b_gu, b_down, layer, p_e, p_row, p_chunks, used_rows, tf):
    n_slots, d = xb.shape
    n_l, n_e, f, _ = w_down.shape
    nf = f // tf
    n_items = p_e.shape[0]

    def jeff(i, j, nch):
        return jnp.where(nch[i] > 0, j, nf - 1)

    return pl.pallas_call(
        _expert_body,
        grid_spec=pltpu.PrefetchScalarGridSpec(
            num_scalar_prefetch=4,
            grid=(n_items, nf),
            in_specs=[
                pl.BlockSpec(memory_space=pl.ANY),
                pl.BlockSpec((None, None, d, 2 * tf), lambda i, j, e, row, nch, used: (layer, e[i], 0, jeff(i, j, nch))),
                pl.BlockSpec((None, None, tf, d), lambda i, j, e, row, nch, used: (layer, e[i], jeff(i, j, nch), 0)),
                pl.BlockSpec((None, None, 1, 2 * tf), lambda i, j, e, row, nch, used: (layer, e[i], 0, jeff(i, j, nch))),
                pl.BlockSpec((None, None, 1, d), lambda i, j, e, row, nch, used: (layer, e[i], 0, 0)),
            ],
            out_specs=pl.BlockSpec(memory_space=pl.ANY),
            scratch_shapes=[pltpu.VMEM((2, MOE_PASS, d), BF16), pltpu.VMEM((MOE_PASS, d), F32),
                            pltpu.VMEM((2, MOE_SUB, 2 * tf), F32), pltpu.VMEM((d, 2 * tf), BF16),
                            pltpu.VMEM((d // LANE, tf, LANE), F32), pltpu.VMEM((tf, d), BF16),
                            pltpu.VMEM((2, MOE_SUB * ROW_TILES, LANE), F32),
                            pltpu.SemaphoreType.DMA((2,)), pltpu.SemaphoreType.DMA((2,))],
        ),
        out_shape=jax.ShapeDtypeStruct((n_slots * ROW_TILES, LANE), F32),
        compiler_params=_params("arbitrary", "arbitrary"),
        name="moe_experts",
    )(p_e, p_row, p_chunks, used_rows, xb, w_gu, w_down, b_gu.reshape(n_l, n_e, 1, 2 * f),
      b_down.reshape(n_l, n_e, 1, d))


def _combine_body(pos_ref, x_ref, gate_ref, y_hbm, g_ref, b_ref, o_ref, buf_ref, y_ref, sem, *, alpha):
    tm = x_ref.shape[0]

    for k in range(TOP_K):
        def start(t, carry, k=k):
            src = pl.ds(pl.multiple_of(pos_ref[k, t] * ROW_TILES, ROW_TILES), ROW_TILES)
            dst = pl.ds(pl.multiple_of((k * tm + t) * ROW_PITCH, 8), ROW_TILES)
            pltpu.make_async_copy(y_hbm.at[src], buf_ref.at[dst], sem).start()
            return carry

        lax.fori_loop(0, tm, start, 0, unroll=8)
    all_rows = pl.ds(0, TOP_K * tm * ROW_TILES)
    pltpu.make_async_copy(y_hbm.at[all_rows], buf_ref.at[all_rows], sem).wait()
    gate = gate_ref[...]
    parts = [_load_row_major(buf_ref, k * tm * ROW_PITCH, tm, ROW_PITCH) for k in range(TOP_K)]
    for c in range(ROW_TILES):
        y_ref[:, c * LANE:(c + 1) * LANE] = (
            (gate[:, 0:1] * parts[0][c] + gate[:, 1:2] * parts[1][c])
            + (gate[:, 2:3] * parts[2][c] + gate[:, 3:4] * parts[3][c]))
    o_ref[...] = _layer_norm_rows(alpha * x_ref[...] + y_ref[...], g_ref[...], b_ref[...])


def _combine_ln(x, yb_rows, pos, gate, g, b, alpha, tm):
    m, d = x.shape
    nt = m // tm
    pos_t = pos.reshape(nt, tm, TOP_K).transpose(0, 2, 1)
    return pl.pallas_call(
        functools.partial(_combine_body, alpha=alpha),
        grid=(nt,),
        in_specs=[pl.BlockSpec((None, TOP_K, tm), lambda i: (i, 0, 0), memory_space=pltpu.SMEM),
                  pl.BlockSpec((tm, d), lambda i: (i, 0)),
                  pl.BlockSpec((tm, TOP_K), lambda i: (i, 0)),
                  pl.BlockSpec(memory_space=pl.ANY),
                  pl.BlockSpec((1, d), lambda i: (0, 0)), pl.BlockSpec((1, d), lambda i: (0, 0))],
        out_specs=pl.BlockSpec((tm, d), lambda i: (i, 0)),
        out_shape=jax.ShapeDtypeStruct((m, d), F32),
        scratch_shapes=[pltpu.VMEM((TOP_K * tm * ROW_PITCH, LANE), F32), pltpu.VMEM((tm, d), F32),
                        pltpu.SemaphoreType.DMA(())],
        compiler_params=_params("arbitrary"),
        name="moe_combine_ln",
    )(pos_t, x, gate, yb_rows, g.reshape(1, d), b.reshape(1, d))


def _mixer(x, w_in, conv_w, w_gate2, b_gate2, gla_norm_g, w_out, ln_g, ln_b, alpha, bsz, seq):
    d = x.shape[1]
    ch = conv_w.shape[1]
    dk_total = w_gate2.shape[1]
    dk = dk_total // GLA_HEADS
    dv = gla_norm_g.shape[1]
    main_cols = w_in.shape[1] - GLA_GATE_RANK
    proj = _matmul(x, w_in[:, :main_cols].astype(BF16), tm=1024, tn=512, name="mixer_in_proj")
    w1p = jnp.zeros((d, LANE), BF16).at[:, :GLA_GATE_RANK].set(w_in[:, main_cols:].astype(BF16))
    w2p = jnp.zeros((LANE, dk_total), BF16).at[:GLA_GATE_RANK, :].set(w_gate2.astype(BF16))
    log_a = _gla_gate(x, w1p, w2p, b_gate2.reshape(1, dk_total), tm=1024)
    y_conv = _short_conv(proj, conv_w, bsz, seq, ch, tt=512, cw=512)
    q_off = 3 * ch
    k_off = q_off + dk_total
    v_off = k_off + dk_total
    r_off = v_off + GLA_HEADS * dv
    y_gla = _gla(proj, log_a, gla_norm_g, bsz, seq, q_off, k_off, v_off, r_off, dk, dv, tt=512)
    w_out_bf = w_out.astype(BF16)
    return _proj_ln(y_conv, y_gla, w_out_bf[:ch], w_out_bf[ch:], x, ln_g, ln_b, alpha, tm=512)


def _moe(x, x_rows, layer, w_router, b_router, w_gu, b_gu, w_down, b_down, ln_g, ln_b, alpha):
    n_e = w_router.shape[1]
    idx_t, gate_t = _router(x, w_router.T.astype(BF16), b_router, tm=1024)
    pos, slot_tok, used_rows, p_e, p_row, p_chunks, n_slots = _dispatch_tables(idx_t, n_e)
    xb = _dispatch_gather(x_rows, slot_tok, used_rows, n_slots)
    yb = _experts(xb, w_gu, w_down, b_gu, b_down, layer, p_e, p_row, p_chunks, used_rows, tf=256)
    return _combine_ln(x, yb, pos, gate_t.T, ln_g, ln_b, alpha, tm=128)


def kernel(x, mem, w_in, conv_w, w_gate2, b_gate2, gla_norm_g, w_out, ln_mix_g, ln_mix_b, w_xq, w_xkv, w_xo, ln_xa_g, ln_xa_b, w_router, b_router, w_gu, b_gu, w_down, b_down, ln_moe_g, ln_moe_b):
    bsz, seq, d = x.shape
    depth = w_in.shape[0]
    alpha = (2 * depth) ** 0.25
    xf = x.reshape(bsz * seq, d)
    memf = mem.reshape(bsz * mem.shape[1], d)
    for l in range(depth):
        xf = _mixer(xf, w_in[l], conv_w[l], w_gate2[l], b_gate2[l], gla_norm_g[l], w_out[l],
                    ln_mix_g[l], ln_mix_b[l], alpha, bsz, seq)
        kv = _matmul(memf, w_xkv[l].astype(BF16), tm=memf.shape[0], tn=512, name="xattn_kv_proj")
        xf, x_rows = _cross_attn(xf, kv, w_xq[l].astype(BF16), w_xo[l].astype(BF16), ln_xa_g[l], ln_xa_b[l],
                         alpha, bsz, seq, tm=512)
        xf = _moe(xf, x_rows, l, w_router[l], b_router[l], w_gu, b_gu, w_down, b_down,
                  ln_moe_g[l], ln_moe_b[l], alpha)
    return xf.reshape(bsz, seq, d)
```

```python
import functools

import numpy as np
import jax
import jax.numpy as jnp
from jax import lax
from jax.experimental import pallas as pl
from jax.experimental.pallas import tpu as pltpu

CONV_W = 3
GLA_HEADS = 4
GLA_GATE_RANK = 16
GLA_TAU = 16.0
XA_HEADS = 4
TOP_K = 4
SWIGLU_LIMIT = 7.0
SWIGLU_ALPHA = 1.702
LN_EPS = 1e-5
NORM_EPS = 1e-6

LANE = 128
GLA_CHUNK = 128
GLA_LEVELS = 7
MOE_SUB = 256
MOE_PASS = 1536
ROW_TILES = 16
ROW_PITCH = 24
VMEM_LIMIT = 56 * 1024 * 1024

BF16 = jnp.bfloat16
F32 = jnp.float32


def _dot(a, b):
    return jnp.dot(a, b, preferred_element_type=F32)


def _dot_nt(a, b):
    return lax.dot_general(a, b, (((1,), (1,)), ((), ())), preferred_element_type=F32)


def _dot_tn(a, b):
    return lax.dot_general(a, b, (((0,), (0,)), ((), ())), preferred_element_type=F32)


def _params(*sem):
    return pltpu.CompilerParams(dimension_semantics=sem, vmem_limit_bytes=VMEM_LIMIT)


def _store_row_major(dst_ref, val):
    n = val.shape[0]
    for c in range(ROW_TILES):
        dst_ref[pl.ds(c, n, stride=ROW_TILES), :] = val[:, c * LANE:(c + 1) * LANE]


def _load_row_major(src_ref, first, n, pitch):
    return [src_ref[pl.ds(first + c, n, stride=pitch), :] for c in range(ROW_TILES)]


def _layer_norm_rows(y, g, b):
    mu = jnp.mean(y, axis=-1, keepdims=True)
    yc = y - mu
    var = jnp.mean(yc * yc, axis=-1, keepdims=True)
    return yc * lax.rsqrt(var + LN_EPS) * g + b


def _matmul_body(a_ref, w_ref, o_ref, abf_ref):
    @pl.when(pl.program_id(1) == 0)
    def _():
        abf_ref[...] = a_ref[...].astype(BF16)

    o_ref[...] = _dot(abf_ref[...], w_ref[...]).astype(o_ref.dtype)


def _matmul(a, w, tm, tn, name):
    m, k = a.shape
    n = w.shape[1]
    return pl.pallas_call(
        _matmul_body,
        grid=(m // tm, n // tn),
        in_specs=[pl.BlockSpec((tm, k), lambda i, j: (i, 0)),
                  pl.BlockSpec((k, tn), lambda i, j: (0, j))],
        out_specs=pl.BlockSpec((tm, tn), lambda i, j: (i, j)),
        out_shape=jax.ShapeDtypeStruct((m, n), F32),
        scratch_shapes=[pltpu.VMEM((tm, k), BF16)],
        compiler_params=_params("parallel", "arbitrary"),
        name=name,
    )(a, w)


def _gate_body(x_ref, w1_ref, w2_ref, b_ref, o_ref):
    low = _dot(x_ref[...].astype(BF16), w1_ref[...])
    z = _dot(low.astype(BF16), w2_ref[...]) + b_ref[...]
    log_sig = jnp.minimum(z, 0.0) - jnp.log1p(jnp.exp(-jnp.abs(z)))
    o_ref[...] = log_sig / GLA_TAU


def _gla_gate(x, w1p, w2p, b2, tm):
    m, d = x.shape
    n = w2p.shape[1]
    return pl.pallas_call(
        _gate_body,
        grid=(m // tm,),
        in_specs=[pl.BlockSpec((tm, d), lambda i: (i, 0)),
                  pl.BlockSpec(w1p.shape, lambda i: (0, 0)),
                  pl.BlockSpec(w2p.shape, lambda i: (0, 0)),
                  pl.BlockSpec((1, n), lambda i: (0, 0))],
        out_specs=pl.BlockSpec((tm, n), lambda i: (i, 0)),
        out_shape=jax.ShapeDtypeStruct((m, n), F32),
        compiler_params=_params("parallel"),
        name="gla_gate",
    )(x, w1p, w2p, b2)


def _conv_body(gc_ref, gb_ref, h_ref, gch_ref, hh_ref, w_ref, o_ref):
    u = gc_ref[...] * h_ref[...]
    uh = jnp.where(pl.program_id(1) == 0, 0.0, gch_ref[...] * hh_ref[...])
    row = lax.broadcasted_iota(jnp.int32, u.shape, 0)
    u1 = jnp.where(row == 0, uh[7:8, :], pltpu.roll(u, 1, 0))
    u2 = pltpu.roll(u, 2, 0)
    u2 = jnp.where(row == 0, uh[6:7, :], jnp.where(row == 1, uh[7:8, :], u2))
    w = w_ref[...]
    conv = w[0:1, :] * u2 + w[1:2, :] * u1 + w[2:3, :] * u
    o_ref[...] = (gb_ref[...] * conv).astype(o_ref.dtype)


def _short_conv(proj, conv_w, bsz, seq, ch, tt, cw):
    nt, nc = seq // tt, ch // cw

    def main(off):
        return pl.BlockSpec((tt, cw), lambda b, i, c: (b * nt + i, off * nc + c))

    def halo(off):
        return pl.BlockSpec(
            (8, cw), lambda b, i, c: (jnp.maximum((b * seq + i * tt) // 8 - 1, 0), off * nc + c))

    return pl.pallas_call(
        _conv_body,
        grid=(bsz, nt, nc),
        in_specs=[main(0), main(1), main(2), halo(0), halo(2),
                  pl.BlockSpec((CONV_W, cw), lambda b, i, c: (0, c))],
        out_specs=pl.BlockSpec((tt, cw), lambda b, i, c: (b * nt + i, c)),
        out_shape=jax.ShapeDtypeStruct((bsz * seq, ch), BF16),
        compiler_params=_params("parallel", "parallel", "parallel"),
        name="short_conv",
    )(proj, proj, proj, proj, proj, conv_w)


def _gla_tables():
    c = GLA_CHUNK
    i = np.arange(c)[:, None]
    t = np.arange(c)[None, :]
    mats = []
    for lvl in range(GLA_LEVELS):
        m = ((i >> (lvl + 1)) << (lvl + 1)) + (1 << lvl) - 1
        upper = (i > m) & (t > m) & (t <= i)
        lower = (i <= m) & (t > i) & (t <= m)
        mats.append(upper | lower)
    mats.append(t <= i)
    return jnp.asarray(np.concatenate(mats, axis=0).astype(np.float32), dtype=BF16)


def _gla_body(q_ref, k_ref, g_ref, v_ref, r_ref, gn_ref, tab_ref, o_ref, st_ref, *, dk_scale):
    c = GLA_CHUNK
    dk = q_ref.shape[1]

    @pl.when(pl.program_id(2) == 0)
    def _():
        st_ref[...] = jnp.zeros_like(st_ref)

    ii = lax.broadcasted_iota(jnp.int32, (c, c), 0)
    jj = lax.broadcasted_iota(jnp.int32, (c, c), 1)
    xor = ii ^ jj
    lower = ii > jj

    for ch in range(q_ref.shape[0] // c):
        rows = pl.ds(ch * c, c)
        q = q_ref[rows, :] * dk_scale
        k = k_ref[rows, :]
        g = g_ref[rows, :]
        v_bf = v_ref[rows, :].astype(BF16)
        g_hi = g.astype(BF16)
        g_lo = (g - g_hi.astype(F32)).astype(BF16)
        sums = _dot(tab_ref[...], jnp.concatenate([g_hi, g_lo], axis=1))
        sums = sums[:, :dk] + sums[:, dk:]

        scores = jnp.where(ii == jj, _dot_nt(q.astype(BF16), k.astype(BF16)), 0.0)
        for lvl in range(GLA_LEVELS):
            w = jnp.exp(sums[lvl * c:(lvl + 1) * c, :])
            s_l = _dot_nt((q * w).astype(BF16), (k * w).astype(BF16))
            scores = jnp.where(((xor >> lvl) == 1) & lower, s_l, scores)

        b = sums[GLA_LEVELS * c:, :]
        b_last = b[c - 1:c, :]
        st = st_ref[...]
        o = _dot_nt((q * jnp.exp(b)).astype(BF16), st.astype(BF16)) + _dot(scores.astype(BF16), v_bf)
        k_dec = (k * jnp.exp(b_last - b)).astype(BF16)
        st_ref[...] = st * jnp.exp(b_last) + _dot_tn(v_bf, k_dec)

        o = o * lax.rsqrt(jnp.mean(o * o, axis=-1, keepdims=True) + NORM_EPS) * gn_ref[...]
        r = r_ref[rows, :]
        o_ref[rows, :] = (o * (r / (1.0 + jnp.exp(-r)))).astype(o_ref.dtype)


def _gla(proj, log_a, gn, bsz, seq, q_off, k_off, v_off, r_off, dk, dv, tt):
    nt = seq // tt
    tab = _gla_tables()

    def spec(width, off):
        return pl.BlockSpec((tt, width), lambda b, h, i: (b * nt + i, off // width + h))

    return pl.pallas_call(
        functools.partial(_gla_body, dk_scale=dk ** -0.5),
        grid=(bsz, GLA_HEADS, nt),
        in_specs=[spec(dk, q_off), spec(dk, k_off), spec(dk, 0), spec(dv, v_off), spec(dv, r_off),
                  pl.BlockSpec((None, 1, dv), lambda b, h, i: (h, 0, 0)),
                  pl.BlockSpec(tab.shape, lambda b, h, i: (0, 0))],
        out_specs=pl.BlockSpec((tt, dv), lambda b, h, i: (b * nt + i, h)),
        out_shape=jax.ShapeDtypeStruct((bsz * seq, GLA_HEADS * dv), BF16),
        scratch_shapes=[pltpu.VMEM((dv, dk), F32)],
        compiler_params=_params("parallel", "parallel", "arbitrary"),
        name="gla",
    )(proj, proj, log_a, proj, proj, gn.reshape(GLA_HEADS, 1, dv), tab)


def _proj_ln_body(a0_ref, a1_ref, w0_ref, w1_ref, x_ref, g_ref, b_ref, o_ref, *, alpha):
    y = _dot(a0_ref[...], w0_ref[...]) + _dot(a1_ref[...], w1_ref[...])
    o_ref[...] = _layer_norm_rows(alpha * x_ref[...] + y, g_ref[...], b_ref[...])


def _proj_ln(a0, a1, w0, w1, x, g, b, alpha, tm):
    m, d = x.shape
    row = lambda i: (i, 0)
    fixed = lambda i: (0, 0)
    return pl.pallas_call(
        functools.partial(_proj_ln_body, alpha=alpha),
        grid=(m // tm,),
        in_specs=[pl.BlockSpec((tm, a0.shape[1]), row), pl.BlockSpec((tm, a1.shape[1]), row),
                  pl.BlockSpec(w0.shape, fixed), pl.BlockSpec(w1.shape, fixed),
                  pl.BlockSpec((tm, d), row), pl.BlockSpec((1, d), fixed), pl.BlockSpec((1, d), fixed)],
        out_specs=pl.BlockSpec((tm, d), row),
        out_shape=jax.ShapeDtypeStruct((m, d), F32),
        compiler_params=_params("parallel"),
        name="mixer_out_ln",
    )(a0, a1, w0, w1, x, g.reshape(1, d), b.reshape(1, d))


def _xattn_body(x_ref, kv_ref, wq_ref, wo_ref, g_ref, b_ref, o_ref, rows_ref, *, alpha):
    x = x_ref[...]
    xa = wq_ref.shape[1]
    hd = xa // XA_HEADS
    q = _dot(x.astype(BF16), wq_ref[...])
    heads = []
    for h in range(XA_HEADS):
        qh = q[:, h * hd:(h + 1) * hd].astype(BF16)
        kh = kv_ref[:, h * hd:(h + 1) * hd].astype(BF16)
        vh = kv_ref[:, xa + h * hd:xa + (h + 1) * hd].astype(BF16)
        s = _dot_nt(qh, kh) * hd ** -0.5
        p = jnp.exp(s - jnp.max(s, axis=-1, keepdims=True))
        p = p / jnp.sum(p, axis=-1, keepdims=True)
        heads.append(_dot(p.astype(BF16), vh))
    o = jnp.concatenate(heads, axis=1).astype(BF16)
    y = _dot(o, wo_ref[...])
    out = _layer_norm_rows(alpha * x + y, g_ref[...], b_ref[...])
    o_ref[...] = out
    _store_row_major(rows_ref, out)


def _cross_attn(x, kv, wq, wo, g, b, alpha, bsz, seq, tm):
    m, d = x.shape
    nt = seq // tm
    mem_len = kv.shape[0] // bsz
    fixed = lambda bb, i: (0, 0)
    return pl.pallas_call(
        functools.partial(_xattn_body, alpha=alpha),
        grid=(bsz, nt),
        in_specs=[pl.BlockSpec((tm, d), lambda bb, i: (bb * nt + i, 0)),
                  pl.BlockSpec((mem_len, kv.shape[1]), lambda bb, i: (bb, 0)),
                  pl.BlockSpec(wq.shape, fixed), pl.BlockSpec(wo.shape, fixed),
                  pl.BlockSpec((1, d), fixed), pl.BlockSpec((1, d), fixed)],
        out_specs=[pl.BlockSpec((tm, d), lambda bb, i: (bb * nt + i, 0)),
                   pl.BlockSpec((tm * ROW_TILES, LANE), lambda bb, i: (bb * nt + i, 0))],
        out_shape=[jax.ShapeDtypeStruct((m, d), F32),
                   jax.ShapeDtypeStruct((m * ROW_TILES, LANE), F32)],
        compiler_params=_params("parallel", "parallel"),
        name="cross_attn_ln",
    )(x, kv, wq, wo, g.reshape(1, d), b.reshape(1, d))


def _router_body(x_ref, wt_ref, b_ref, idx_ref, gate_ref):
    logits = _dot_nt(wt_ref[...], x_ref[...].astype(BF16)) + b_ref[...]
    n_e = logits.shape[0]
    e_iota = lax.broadcasted_iota(jnp.int32, logits.shape, 0)
    vals, idxs = [], []
    for _ in range(TOP_K):
        best = jnp.max(logits, axis=0, keepdims=True)
        sel = jnp.min(jnp.where(logits == best, e_iota, n_e), axis=0, keepdims=True)
        vals.append(best)
        idxs.append(sel)
        logits = jnp.where(e_iota == sel, -jnp.inf, logits)
    exps = [jnp.exp(v - vals[0]) for v in vals]
    total = exps[0] + exps[1] + exps[2] + exps[3]
    idx_ref[...] = jnp.concatenate(idxs, axis=0)
    gate_ref[...] = jnp.concatenate([e / total for e in exps], axis=0)


def _router(x, wt, b, tm):
    m, d = x.shape
    n_e = wt.shape[0]
    return pl.pallas_call(
        _router_body,
        grid=(m // tm,),
        in_specs=[pl.BlockSpec((tm, d), lambda i: (i, 0)),
                  pl.BlockSpec((n_e, d), lambda i: (0, 0)),
                  pl.BlockSpec((n_e, 1), lambda i: (0, 0))],
        out_specs=[pl.BlockSpec((TOP_K, tm), lambda i: (0, i)),
                   pl.BlockSpec((TOP_K, tm), lambda i: (0, i))],
        out_shape=[jax.ShapeDtypeStruct((TOP_K, m), jnp.int32),
                   jax.ShapeDtypeStruct((TOP_K, m), F32)],
        compiler_params=_params("parallel"),
        name="router_topk",
    )(x, wt, b.reshape(n_e, 1))


def _dispatch_tables(idx_t, n_experts):
    n_tok = idx_t.shape[1]
    n_asg = n_tok * TOP_K
    n_slots = n_asg + n_experts * MOE_SUB
    n_pass_max = n_asg // MOE_PASS + n_experts
    flat_e = idx_t.T.reshape(n_asg)
    onehot = (flat_e[:, None] == jnp.arange(n_experts, dtype=jnp.int32)[None, :]).astype(jnp.int32)
    csum = jnp.cumsum(onehot, axis=0)
    counts = csum[-1]
    rank = jnp.sum(csum * onehot, axis=1) - 1
    padded = (counts + MOE_SUB - 1) // MOE_SUB * MOE_SUB
    row_end = jnp.cumsum(padded)
    row_start = row_end - padded
    pos = (row_start[flat_e] + rank).astype(jnp.int32)
    asg = jnp.arange(n_asg, dtype=jnp.int32)
    slot_tok = jnp.zeros((n_slots,), jnp.int32).at[pos].set(asg // TOP_K)
    n_pass = (padded + MOE_PASS - 1) // MOE_PASS
    pass_end = jnp.cumsum(n_pass)
    pass_start = pass_end - n_pass
    total = pass_end[-1]
    i = jnp.arange(n_pass_max, dtype=jnp.int32)
    i_eff = jnp.minimum(i, total - 1)
    p_e = jnp.clip(jnp.searchsorted(pass_end, i_eff, side="right"), 0, n_experts - 1).astype(jnp.int32)
    p_off = (i_eff - pass_start[p_e]) * MOE_PASS
    p_row = (row_start[p_e] + p_off).astype(jnp.int32)
    p_chunks = jnp.clip((padded[p_e] - p_off) // MOE_SUB, 0, MOE_PASS // MOE_SUB)
    p_chunks = jnp.where(i < total, p_chunks, 0).astype(jnp.int32)
    used_rows = row_end[-1].astype(jnp.int32).reshape(1)
    return pos, slot_tok, used_rows, p_e, p_row, p_chunks, n_slots


def _gather_body(used_ref, tok_ref, tok_next_ref, x_hbm, o_ref, buf_ref, sem):
    j = pl.program_id(0)
    slot = j % 2

    def issue(toks, s):
        def pair(p, carry):
            for prio in range(2):
                r = 2 * p + prio
                src = pl.ds(pl.multiple_of(toks[0, r] * ROW_TILES, ROW_TILES), ROW_TILES)
                dst = pl.ds(pl.multiple_of(r * ROW_PITCH, 8), ROW_TILES)
                pltpu.make_async_copy(x_hbm.at[src], buf_ref.at[s, dst], sem.at[s]).start(priority=prio)
            return carry

        lax.fori_loop(0, MOE_SUB // 2, pair, 0, unroll=4)

    @pl.when(j == 0)
    def _():
        issue(tok_ref, 0)

    @pl.when((j + 1) * MOE_SUB < used_ref[0])
    def _():
        issue(tok_next_ref, 1 - slot)

    @pl.when(j * MOE_SUB < used_ref[0])
    def _():
        all_rows = pl.ds(0, MOE_SUB * ROW_TILES)
        pltpu.make_async_copy(x_hbm.at[all_rows], buf_ref.at[slot, all_rows], sem.at[slot]).wait()
        for c, part in enumerate(_load_row_major(buf_ref.at[slot], 0, MOE_SUB, ROW_PITCH)):
            o_ref[:, c * LANE:(c + 1) * LANE] = part.astype(o_ref.dtype)

    @pl.when(j * MOE_SUB >= used_ref[0])
    def _():
        o_ref[...] = jnp.zeros_like(o_ref)


def _dispatch_gather(x_rows, slot_tok, used_rows, n_slots):
    d = ROW_TILES * LANE
    n_blk = n_slots // MOE_SUB
    toks = slot_tok.reshape(n_blk, 1, MOE_SUB)
    return pl.pallas_call(
        _gather_body,
        grid_spec=pltpu.PrefetchScalarGridSpec(
            num_scalar_prefetch=1,
            grid=(n_blk,),
            in_specs=[pl.BlockSpec((None, 1, MOE_SUB), lambda j, used: (j, 0, 0),
                                   memory_space=pltpu.SMEM),
                      pl.BlockSpec((None, 1, MOE_SUB), lambda j, used: (jnp.minimum(j + 1, n_blk - 1), 0, 0),
                                   memory_space=pltpu.SMEM),
                      pl.BlockSpec(memory_space=pl.ANY)],
            out_specs=pl.BlockSpec((MOE_SUB, d), lambda j, used: (j, 0)),
            scratch_shapes=[pltpu.VMEM((2, MOE_SUB * ROW_PITCH, LANE), F32), pltpu.SemaphoreType.DMA((2,))],
        ),
        out_shape=jax.ShapeDtypeStruct((n_slots, d), BF16),
        compiler_params=_params("arbitrary"),
        name="moe_gather",
    )(used_rows, toks, toks, x_rows)


def _expert_body(e_ref, row_ref, nch_ref, used_ref, x_hbm, wgu_ref, wd_ref, bgu_ref, bd_ref, y_hbm,
                 x_buf, acc, gu_buf, wgu_bf, wd_perm, wd_bf, y_stage, x_sem, y_sem):
    i, j = pl.program_id(0), pl.program_id(1)
    n_items, last = pl.num_programs(0), pl.num_programs(1) - 1
    nch = nch_ref[i]
    slot = i % 2
    tf = wd_ref.shape[0]
    half = tf // 2

    def chunk(c):
        return pl.ds(pl.multiple_of(c * MOE_SUB, MOE_SUB), MOE_SUB)

    def x_copy(item, c):
        rows = pl.ds(pl.multiple_of(row_ref[item] + c * MOE_SUB, MOE_SUB), MOE_SUB)
        return pltpu.make_async_copy(x_hbm.at[rows], x_buf.at[item % 2, chunk(c)], x_sem.at[item % 2])

    def y_copy(first_row, s):
        rows = pl.ds(pl.multiple_of(first_row * ROW_TILES, MOE_SUB * ROW_TILES), MOE_SUB * ROW_TILES)
        return pltpu.make_async_copy(y_stage.at[s], y_hbm.at[rows], y_sem.at[s])

    def y_drain(n_chunks):
        for s in range(2):
            @pl.when(n_chunks > s)
            def _():
                y_copy(0, s).wait()

    def for_chunks(item, fn):
        def step(c, carry):
            fn(item, c)
            return carry

        lax.fori_loop(0, nch_ref[item], step, 0)

    @pl.when(j == 0)
    def _():
        @pl.when(i == 0)
        def _():
            for_chunks(0, lambda it, c: x_copy(it, c).start())

        @pl.when(i + 1 < n_items)
        def _():
            for_chunks(i + 1, lambda it, c: x_copy(it, c).start())

        for_chunks(i, lambda it, c: x_copy(it, c).wait())
        acc[...] = jnp.broadcast_to(bd_ref[...], acc.shape)

    even_lane = lax.broadcasted_iota(jnp.int32, (1, tf), 1) % 2 == 0

    def up_proj(c):
        return _dot(x_buf[slot, chunk(c), :], wgu_bf[...]) + bgu_ref[...]

    def down_proj(gu):
        up = jnp.clip(pltpu.roll(gu, 2 * tf - 1, 1), -SWIGLU_LIMIT, SWIGLU_LIMIT)
        gate = jnp.minimum(gu, SWIGLU_LIMIT)
        act = (up + 1.0) * (gate / (1.0 + jnp.exp(-SWIGLU_ALPHA * gate)))
        packed = jnp.where(even_lane, act[:, :tf], pltpu.roll(act[:, tf:], 1, 1))
        return _dot(packed.astype(BF16), wd_bf[...])

    def run_chunks(emit):
        wgu_bf[...] = wgu_ref[...].astype(BF16)
        for cb in range(wd_ref.shape[1] // LANE):
            cols = pl.ds(cb * LANE, LANE)
            wd_perm[cb, pl.ds(0, half, stride=2), :] = wd_ref[0:half, cols]
            wd_perm[cb, pl.ds(1, half, stride=2), :] = wd_ref[half:tf, cols]
            wd_bf[:, cols] = wd_perm[cb].astype(BF16)

        def reserve_stage(c):
            if emit:
                @pl.when(c >= 2)
                def _():
                    y_copy(0, c % 2).wait()

        def finish(c, gu):
            y = down_proj(gu)
            if emit:
                _store_row_major(y_stage.at[c % 2], acc[chunk(c), :] + y)
                y_copy(row_ref[i] + c * MOE_SUB, c % 2).start()
            else:
                acc[chunk(c), :] += y

        gu_buf[0] = up_proj(0)

        def step(c, carry):
            reserve_stage(c)
            gu = gu_buf[c % 2]
            gu_next = up_proj(c + 1)
            finish(c, gu)
            gu_buf[(c + 1) % 2] = gu_next
            return carry

        lax.fori_loop(0, nch - 1, step, 0)
        reserve_stage(nch - 1)
        finish(nch - 1, gu_buf[(nch - 1) % 2])

    @pl.when((nch > 0) & (j < last))
    def _():
        run_chunks(False)

    @pl.when(j == last)
    def _():
        @pl.when(i > 0)
        def _():
            y_drain(nch_ref[i - 1])

        @pl.when(nch > 0)
        def _():
            run_chunks(True)

        @pl.when(i == n_items - 1)
        def _():
            y_drain(nch)
            y_stage[0] = jnp.zeros(y_stage.shape[1:], y_stage.dtype)

            def fill_start(c, carry):
                y_copy(c * MOE_SUB, 0).start()
                return carry

            def fill_wait(c, carry):
                y_copy(0, 0).wait()
                return carry

            first, end = used_ref[0] // MOE_SUB, y_hbm.shape[0] // (MOE_SUB * ROW_TILES)
            lax.fori_loop(first, end, fill_start, 0)
            lax.fori_loop(first, end, fill_wait, 0)


def _experts(xb, w_gu, w_down, b_gu, b_down, layer, p_e, p_row, p_chunks, used_rows, tf):
    n_slots, d = xb.shape
    n_l, n_e, f, _ = w_down.shape
    nf = f // tf
    n_items = p_e.shape[0]

    def jeff(i, j, nch):
        return jnp.where(nch[i] > 0, j, nf - 1)

    return pl.pallas_call(
        _expert_body,
        grid_spec=pltpu.PrefetchScalarGridSpec(
            num_scalar_prefetch=4,
            grid=(n_items, nf),
            in_specs=[
                pl.BlockSpec(memory_space=pl.ANY),
                pl.BlockSpec((None, None, d, 2 * tf), lambda i, j, e, row, nch, used: (layer, e[i], 0, jeff(i, j, nch))),
                pl.BlockSpec((None, None, tf, d), lambda i, j, e, row, nch, used: (layer, e[i], jeff(i, j, nch), 0)),
                pl.BlockSpec((None, None, 1, 2 * tf), lambda i, j, e, row, nch, used: (layer, e[i], 0, jeff(i, j, nch))),
                pl.BlockSpec((None, None, 1, d), lambda i, j, e, row, nch, used: (layer, e[i], 0, 0)),
            ],
            out_specs=pl.BlockSpec(memory_space=pl.ANY),
            scratch_shapes=[pltpu.VMEM((2, MOE_PASS, d), BF16), pltpu.VMEM((MOE_PASS, d), F32),
                            pltpu.VMEM((2, MOE_SUB, 2 * tf), F32), pltpu.VMEM((d, 2 * tf), BF16),
                            pltpu.VMEM((d // LANE, tf, LANE), F32), pltpu.VMEM((tf, d), BF16),
                            pltpu.VMEM((2, MOE_SUB * ROW_TILES, LANE), F32),
                            pltpu.SemaphoreType.DMA((2,)), pltpu.SemaphoreType.DMA((2,))],
        ),
        out_shape=jax.ShapeDtypeStruct((n_slots * ROW_TILES, LANE), F32),
        compiler_params=_params("arbitrary", "arbitrary"),
        name="moe_experts",
    )(p_e, p_row, p_chunks, used_rows, xb, w_gu, w_down, b_gu.reshape(n_l, n_e, 1, 2 * f),
      b_down.reshape(n_l, n_e, 1, d))


def _combine_body(pos_ref, pos_next_ref, x_ref, gate_ref, y_hbm, g_ref, b_ref, o_ref, buf_ref, y_ref, sem,
                  *, alpha):
    tm = x_ref.shape[0]
    i = pl.program_id(0)
    slot = i % 2

    def issue(pos, s):
        for k in range(TOP_K):
            def pair(p, carry, k=k):
                for prio in range(2):
                    t = 2 * p + prio
                    src = pl.ds(pl.multiple_of(pos[k, t] * ROW_TILES, ROW_TILES), ROW_TILES)
                    dst = pl.ds(pl.multiple_of((k * tm + t) * ROW_PITCH, 8), ROW_TILES)
                    pltpu.make_async_copy(y_hbm.at[src], buf_ref.at[s, dst], sem.at[s]).start(priority=prio)
                return carry

            lax.fori_loop(0, tm // 2, pair, 0, unroll=4)

    @pl.when(i == 0)
    def _():
        issue(pos_ref, 0)

    @pl.when(i + 1 < pl.num_programs(0))
    def _():
        issue(pos_next_ref, 1 - slot)

    all_rows = pl.ds(0, TOP_K * tm * ROW_TILES)
    pltpu.make_async_copy(y_hbm.at[all_rows], buf_ref.at[slot, all_rows], sem.at[slot]).wait()
    gate = gate_ref[...]
    parts = [_load_row_major(buf_ref.at[slot], k * tm * ROW_PITCH, tm, ROW_PITCH) for k in range(TOP_K)]
    for c in range(ROW_TILES):
        y_ref[:, c * LANE:(c + 1) * LANE] = (
            (gate[:, 0:1] * parts[0][c] + gate[:, 1:2] * parts[1][c])
            + (gate[:, 2:3] * parts[2][c] + gate[:, 3:4] * parts[3][c]))
    o_ref[...] = _layer_norm_rows(alpha * x_ref[...] + y_ref[...], g_ref[...], b_ref[...])


def _combine_ln(x, yb_rows, pos, gate, g, b, alpha, tm):
    m, d = x.shape
    nt = m // tm
    pos_t = pos.reshape(nt, tm, TOP_K).transpose(0, 2, 1)
    return pl.pallas_call(
        functools.partial(_combine_body, alpha=alpha),
        grid=(nt,),
        in_specs=[pl.BlockSpec((None, TOP_K, tm), lambda i: (i, 0, 0), memory_space=pltpu.SMEM),
                  pl.BlockSpec((None, TOP_K, tm), lambda i: (jnp.minimum(i + 1, nt - 1), 0, 0),
                               memory_space=pltpu.SMEM),
                  pl.BlockSpec((tm, d), lambda i: (i, 0)),
                  pl.BlockSpec((tm, TOP_K), lambda i: (i, 0)),
                  pl.BlockSpec(memory_space=pl.ANY),
                  pl.BlockSpec((1, d), lambda i: (0, 0)), pl.BlockSpec((1, d), lambda i: (0, 0))],
        out_specs=pl.BlockSpec((tm, d), lambda i: (i, 0)),
        out_shape=jax.ShapeDtypeStruct((m, d), F32),
        scratch_shapes=[pltpu.VMEM((2, TOP_K * tm * ROW_PITCH, LANE), F32), pltpu.VMEM((tm, d), F32),
                        pltpu.SemaphoreType.DMA((2,))],
        compiler_params=_params("arbitrary"),
        name="moe_combine_ln",
    )(pos_t, pos_t, x, gate, yb_rows, g.reshape(1, d), b.reshape(1, d))


def _mixer(x, w_in, conv_w, w_gate2, b_gate2, gla_norm_g, w_out, ln_g, ln_b, alpha, bsz, seq):
    d = x.shape[1]
    ch = conv_w.shape[1]
    dk_total = w_gate2.shape[1]
    dk = dk_total // GLA_HEADS
    dv = gla_norm_g.shape[1]
    main_cols = w_in.shape[1] - GLA_GATE_RANK
    proj = _matmul(x, w_in[:, :main_cols].astype(BF16), tm=1024, tn=512, name="mixer_in_proj")
    w1p = jnp.zeros((d, LANE), BF16).at[:, :GLA_GATE_RANK].set(w_in[:, main_cols:].astype(BF16))
    w2p = jnp.zeros((LANE, dk_total), BF16).at[:GLA_GATE_RANK, :].set(w_gate2.astype(BF16))
    log_a = _gla_gate(x, w1p, w2p, b_gate2.reshape(1, dk_total), tm=1024)
    y_conv = _short_conv(proj, conv_w, bsz, seq, ch, tt=512, cw=512)
    q_off = 3 * ch
    k_off = q_off + dk_total
    v_off = k_off + dk_total
    r_off = v_off + GLA_HEADS * dv
    y_gla = _gla(proj, log_a, gla_norm_g, bsz, seq, q_off, k_off, v_off, r_off, dk, dv, tt=512)
    w_out_bf = w_out.astype(BF16)
    return _proj_ln(y_conv, y_gla, w_out_bf[:ch], w_out_bf[ch:], x, ln_g, ln_b, alpha, tm=512)


def _moe(x, x_rows, layer, w_router, b_router, w_gu, b_gu, w_down, b_down, ln_g, ln_b, alpha):
    n_e = w_router.shape[1]
    idx_t, gate_t = _router(x, w_router.T.astype(BF16), b_router, tm=1024)
    pos, slot_tok, used_rows, p_e, p_row, p_chunks, n_slots = _dispatch_tables(idx_t, n_e)
    xb = _dispatch_gather(x_rows, slot_tok, used_rows, n_slots)
    yb = _experts(xb, w_gu, w_down, b_gu, b_down, layer, p_e, p_row, p_chunks, used_rows, tf=256)
    return _combine_ln(x, yb, pos, gate_t.T, ln_g, ln_b, alpha, tm=128)


def kernel(x, mem, w_in, conv_w, w_gate2, b_gate2, gla_norm_g, w_out, ln_mix_g, ln_mix_b, w_xq, w_xkv, w_xo, ln_xa_g, ln_xa_b, w_router, b_router, w_gu, b_gu, w_down, b_down, ln_moe_g, ln_moe_b):
    bsz, seq, d = x.shape
    depth = w_in.shape[0]
    alpha = (2 * depth) ** 0.25
    xf = x.reshape(bsz * seq, d)
    memf = mem.reshape(bsz * mem.shape[1], d)
    for l in range(depth):
        xf = _mixer(xf, w_in[l], conv_w[l], w_gate2[l], b_gate2[l], gla_norm_g[l], w_out[l],
                    ln_mix_g[l], ln_mix_b[l], alpha, bsz, seq)
        kv = _matmul(memf, w_xkv[l].astype(BF16), tm=memf.shape[0], tn=512, name="xattn_kv_proj")
        xf, x_rows = _cross_attn(xf, kv, w_xq[l].astype(BF16), w_xo[l].astype(BF16), ln_xa_g[l], ln_xa_b[l],
                         alpha, bsz, seq, tm=512)
        xf = _moe(xf, x_rows, l, w_router[l], b_router[l], w_gu, b_gu, w_down, b_down,
                  ln_moe_g[l], ln_moe_b[l], alpha)
    return xf.reshape(bsz, seq, d)
```

```python
import functools

import numpy as np
import jax
import jax.numpy as jnp
from jax import lax
from jax.experimental import pallas as pl
from jax.experimental.pallas import tpu as pltpu

CONV_W = 3
GLA_HEADS = 4
GLA_GATE_RANK = 16
GLA_TAU = 16.0
XA_HEADS = 4
TOP_K = 4
SWIGLU_LIMIT = 7.0
SWIGLU_ALPHA = 1.702
LN_EPS = 1e-5
NORM_EPS = 1e-6

LANE = 128
GLA_CHUNK = 128
GLA_LEVELS = 7
MOE_SUB = 256
MOE_PASS = 1536
ROW_TILES = 16
ROW_PITCH = 24
VMEM_LIMIT = 56 * 1024 * 1024

BF16 = jnp.bfloat16
F32 = jnp.float32


def _dot(a, b):
    return jnp.dot(a, b, preferred_element_type=F32)


def _dot_nt(a, b):
    return lax.dot_general(a, b, (((1,), (1,)), ((), ())), preferred_element_type=F32)


def _dot_tn(a, b):
    return lax.dot_general(a, b, (((0,), (0,)), ((), ())), preferred_element_type=F32)


def _params(*sem):
    return pltpu.CompilerParams(dimension_semantics=sem, vmem_limit_bytes=VMEM_LIMIT)


def _store_row_major(dst_ref, val):
    n = val.shape[0]
    for c in range(ROW_TILES):
        dst_ref[pl.ds(c, n, stride=ROW_TILES), :] = val[:, c * LANE:(c + 1) * LANE]


def _load_row_major(src_ref, first, n, pitch):
    return [src_ref[pl.ds(first + c, n, stride=pitch), :] for c in range(ROW_TILES)]


def _layer_norm_rows(y, g, b):
    mu = jnp.mean(y, axis=-1, keepdims=True)
    yc = y - mu
    var = jnp.mean(yc * yc, axis=-1, keepdims=True)
    return yc * lax.rsqrt(var + LN_EPS) * g + b


def _matmul_body(a_ref, w_ref, o_ref, abf_ref):
    @pl.when(pl.program_id(1) == 0)
    def _():
        abf_ref[...] = a_ref[...].astype(BF16)

    o_ref[...] = _dot(abf_ref[...], w_ref[...]).astype(o_ref.dtype)


def _matmul(a, w, tm, tn, name):
    m, k = a.shape
    n = w.shape[1]
    return pl.pallas_call(
        _matmul_body,
        grid=(m // tm, n // tn),
        in_specs=[pl.BlockSpec((tm, k), lambda i, j: (i, 0)),
                  pl.BlockSpec((k, tn), lambda i, j: (0, j))],
        out_specs=pl.BlockSpec((tm, tn), lambda i, j: (i, j)),
        out_shape=jax.ShapeDtypeStruct((m, n), F32),
        scratch_shapes=[pltpu.VMEM((tm, k), BF16)],
        compiler_params=_params("parallel", "arbitrary"),
        name=name,
    )(a, w)


def _in_proj_body(a_ref, w_ref, w1_ref, w2_ref, b_ref, o_ref, gate_ref, abf_ref):
    @pl.when(pl.program_id(1) == 0)
    def _():
        abf_ref[...] = a_ref[...].astype(BF16)
        low = _dot(abf_ref[...], w1_ref[...])
        z = _dot(low.astype(BF16), w2_ref[...]) + b_ref[...]
        log_sig = jnp.minimum(z, 0.0) - jnp.log1p(jnp.exp(-jnp.abs(z)))
        gate_ref[...] = log_sig / GLA_TAU

    o_ref[...] = _dot(abf_ref[...], w_ref[...])


def _in_proj(a, w, w1p, w2p, b2, tm, tn):
    m, k = a.shape
    n = w.shape[1]
    ng = w2p.shape[1]
    fixed = lambda i, j: (0, 0)
    return pl.pallas_call(
        _in_proj_body,
        grid=(m // tm, n // tn),
        in_specs=[pl.BlockSpec((tm, k), lambda i, j: (i, 0)),
                  pl.BlockSpec((k, tn), lambda i, j: (0, j)),
                  pl.BlockSpec(w1p.shape, fixed), pl.BlockSpec(w2p.shape, fixed),
                  pl.BlockSpec((1, ng), fixed)],
        out_specs=[pl.BlockSpec((tm, tn), lambda i, j: (i, j)),
                   pl.BlockSpec((tm, ng), lambda i, j: (i, 0))],
        out_shape=[jax.ShapeDtypeStruct((m, n), F32), jax.ShapeDtypeStruct((m, ng), F32)],
        scratch_shapes=[pltpu.VMEM((tm, k), BF16)],
        compiler_params=_params("parallel", "arbitrary"),
        name="mixer_in_proj",
    )(a, w, w1p, w2p, b2)


def _conv_body(gc_ref, gb_ref, h_ref, gch_ref, hh_ref, w_ref, o_ref):
    u = gc_ref[...] * h_ref[...]
    uh = jnp.where(pl.program_id(1) == 0, 0.0, gch_ref[...] * hh_ref[...])
    row = lax.broadcasted_iota(jnp.int32, u.shape, 0)
    u1 = jnp.where(row == 0, uh[7:8, :], pltpu.roll(u, 1, 0))
    u2 = pltpu.roll(u, 2, 0)
    u2 = jnp.where(row == 0, uh[6:7, :], jnp.where(row == 1, uh[7:8, :], u2))
    w = w_ref[...]
    conv = w[0:1, :] * u2 + w[1:2, :] * u1 + w[2:3, :] * u
    o_ref[...] = (gb_ref[...] * conv).astype(o_ref.dtype)


def _short_conv(proj, conv_w, bsz, seq, ch, tt, cw):
    nt, nc = seq // tt, ch // cw

    def main(off):
        return pl.BlockSpec((tt, cw), lambda b, i, c: (b * nt + i, off * nc + c))

    def halo(off):
        return pl.BlockSpec(
            (8, cw), lambda b, i, c: (jnp.maximum((b * seq + i * tt) // 8 - 1, 0), off * nc + c))

    return pl.pallas_call(
        _conv_body,
        grid=(bsz, nt, nc),
        in_specs=[main(0), main(1), main(2), halo(0), halo(2),
                  pl.BlockSpec((CONV_W, cw), lambda b, i, c: (0, c))],
        out_specs=pl.BlockSpec((tt, cw), lambda b, i, c: (b * nt + i, c)),
        out_shape=jax.ShapeDtypeStruct((bsz * seq, ch), BF16),
        compiler_params=_params("parallel", "parallel", "parallel"),
        name="short_conv",
    )(proj, proj, proj, proj, proj, conv_w)


def _gla_tables():
    c = GLA_CHUNK
    i = np.arange(c)[:, None]
    t = np.arange(c)[None, :]
    mats = []
    for lvl in range(GLA_LEVELS):
        m = ((i >> (lvl + 1)) << (lvl + 1)) + (1 << lvl) - 1
        upper = (i > m) & (t > m) & (t <= i)
        lower = (i <= m) & (t > i) & (t <= m)
        mats.append(upper | lower)
    mats.append(t <= i)
    return jnp.asarray(np.concatenate(mats, axis=0).astype(np.float32), dtype=BF16)


def _gla_body(q_ref, k_ref, g_ref, v_ref, r_ref, gn_ref, tab_ref, o_ref, st_ref, *, dk_scale):
    c = GLA_CHUNK
    n_heads = st_ref.shape[0]
    dv, dk = st_ref.shape[1:]

    @pl.when(pl.program_id(2) == 0)
    def _():
        st_ref[...] = jnp.zeros_like(st_ref)

    ii = lax.broadcasted_iota(jnp.int32, (c, c), 0)
    jj = lax.broadcasted_iota(jnp.int32, (c, c), 1)
    xor = ii ^ jj
    lower = ii > jj

    for ch in range(q_ref.shape[0] // c):
        rows = pl.ds(ch * c, c)
        for hh in range(n_heads):
            kcols = pl.ds(hh * dk, dk)
            vcols = pl.ds(hh * dv, dv)
            q = q_ref[rows, kcols] * dk_scale
            k = k_ref[rows, kcols]
            g = g_ref[rows, kcols]
            v_bf = v_ref[rows, vcols].astype(BF16)
            g_hi = g.astype(BF16)
            g_lo = (g - g_hi.astype(F32)).astype(BF16)
            sums = _dot(tab_ref[...], jnp.concatenate([g_hi, g_lo], axis=1))
            sums = sums[:, :dk] + sums[:, dk:]

            scores = jnp.where(ii == jj, _dot_nt(q.astype(BF16), k.astype(BF16)), 0.0)
            for lvl in range(GLA_LEVELS):
                w = jnp.exp(sums[lvl * c:(lvl + 1) * c, :])
                s_l = _dot_nt((q * w).astype(BF16), (k * w).astype(BF16))
                scores = jnp.where(((xor >> lvl) == 1) & lower, s_l, scores)

            b = sums[GLA_LEVELS * c:, :]
            b_last = b[c - 1:c, :]
            st = st_ref[hh]
            o = _dot_nt((q * jnp.exp(b)).astype(BF16), st.astype(BF16)) + _dot(scores.astype(BF16), v_bf)
            k_dec = (k * jnp.exp(b_last - b)).astype(BF16)
            st_ref[hh] = st * jnp.exp(b_last) + _dot_tn(v_bf, k_dec)

            o = o * lax.rsqrt(jnp.mean(o * o, axis=-1, keepdims=True) + NORM_EPS) * gn_ref[hh]
            r = r_ref[rows, vcols]
            o_ref[rows, vcols] = (o * (r / (1.0 + jnp.exp(-r)))).astype(o_ref.dtype)


def _gla(proj, log_a, gn, bsz, seq, q_off, k_off, v_off, r_off, dk, dv, tt, hpb):
    nt = seq // tt
    tab = _gla_tables()

    def spec(width, off):
        return pl.BlockSpec((tt, hpb * width), lambda b, h, i: (b * nt + i, off // (hpb * width) + h))

    return pl.pallas_call(
        functools.partial(_gla_body, dk_scale=dk ** -0.5),
        grid=(bsz, GLA_HEADS // hpb, nt),
        in_specs=[spec(dk, q_off), spec(dk, k_off), spec(dk, 0), spec(dv, v_off), spec(dv, r_off),
                  pl.BlockSpec((hpb, 1, dv), lambda b, h, i: (h, 0, 0)),
                  pl.BlockSpec(tab.shape, lambda b, h, i: (0, 0))],
        out_specs=pl.BlockSpec((tt, hpb * dv), lambda b, h, i: (b * nt + i, h)),
        out_shape=jax.ShapeDtypeStruct((bsz * seq, GLA_HEADS * dv), BF16),
        scratch_shapes=[pltpu.VMEM((hpb, dv, dk), F32)],
        compiler_params=_params("parallel", "parallel", "arbitrary"),
        name="gla",
    )(proj, proj, log_a, proj, proj, gn.reshape(GLA_HEADS, 1, dv), tab)


def _proj_ln_body(a0_ref, a1_ref, w0_ref, w1_ref, x_ref, g_ref, b_ref, o_ref, *, alpha):
    y = _dot(a0_ref[...], w0_ref[...]) + _dot(a1_ref[...], w1_ref[...])
    o_ref[...] = _layer_norm_rows(alpha * x_ref[...] + y, g_ref[...], b_ref[...])


def _proj_ln(a0, a1, w0, w1, x, g, b, alpha, tm):
    m, d = x.shape
    row = lambda i: (i, 0)
    fixed = lambda i: (0, 0)
    return pl.pallas_call(
        functools.partial(_proj_ln_body, alpha=alpha),
        grid=(m // tm,),
        in_specs=[pl.BlockSpec((tm, a0.shape[1]), row), pl.BlockSpec((tm, a1.shape[1]), row),
                  pl.BlockSpec(w0.shape, fixed), pl.BlockSpec(w1.shape, fixed),
                  pl.BlockSpec((tm, d), row), pl.BlockSpec((1, d), fixed), pl.BlockSpec((1, d), fixed)],
        out_specs=pl.BlockSpec((tm, d), row),
        out_shape=jax.ShapeDtypeStruct((m, d), F32),
        compiler_params=_params("parallel"),
        name="mixer_out_ln",
    )(a0, a1, w0, w1, x, g.reshape(1, d), b.reshape(1, d))


def _route(x, wt_ref, b_ref, idx_ref, gate_ref):
    logits = _dot_nt(wt_ref[...], x.astype(BF16)) + b_ref[...]
    n_e = logits.shape[0]
    e_iota = lax.broadcasted_iota(jnp.int32, logits.shape, 0)
    vals, idxs = [], []
    for _ in range(TOP_K):
        best = jnp.max(logits, axis=0, keepdims=True)
        sel = jnp.min(jnp.where(logits == best, e_iota, n_e), axis=0, keepdims=True)
        vals.append(best)
        idxs.append(sel)
        logits = jnp.where(e_iota == sel, -jnp.inf, logits)
    exps = [jnp.exp(v - vals[0]) for v in vals]
    total = exps[0] + exps[1] + exps[2] + exps[3]
    idx_ref[...] = jnp.concatenate(idxs, axis=0)
    gate_ref[...] = jnp.concatenate([e / total for e in exps], axis=0)


def _xattn_body(x_ref, kv_ref, wq_ref, wo_ref, g_ref, b_ref, wr_ref, br_ref, o_ref, rows_ref, idx_ref, gate_ref,
                *, alpha):
    x = x_ref[...]
    xa = wq_ref.shape[1]
    hd = xa // XA_HEADS
    q = _dot(x.astype(BF16), wq_ref[...])
    heads = []
    for h in range(XA_HEADS):
        qh = q[:, h * hd:(h + 1) * hd].astype(BF16)
        kh = kv_ref[:, h * hd:(h + 1) * hd].astype(BF16)
        vh = kv_ref[:, xa + h * hd:xa + (h + 1) * hd].astype(BF16)
        s = _dot_nt(qh, kh) * hd ** -0.5
        p = jnp.exp(s - jnp.max(s, axis=-1, keepdims=True))
        p = p / jnp.sum(p, axis=-1, keepdims=True)
        heads.append(_dot(p.astype(BF16), vh))
    o = jnp.concatenate(heads, axis=1).astype(BF16)
    y = _dot(o, wo_ref[...])
    out = _layer_norm_rows(alpha * x + y, g_ref[...], b_ref[...])
    o_ref[...] = out
    _store_row_major(rows_ref, out)
    _route(out, wr_ref, br_ref, idx_ref, gate_ref)


def _cross_attn(x, kv, wq, wo, g, b, wr_t, b_router, alpha, bsz, seq, tm):
    m, d = x.shape
    n_e = wr_t.shape[0]
    nt = seq // tm
    mem_len = kv.shape[0] // bsz
    fixed = lambda bb, i: (0, 0)
    return pl.pallas_call(
        functools.partial(_xattn_body, alpha=alpha),
        grid=(bsz, nt),
        in_specs=[pl.BlockSpec((tm, d), lambda bb, i: (bb * nt + i, 0)),
                  pl.BlockSpec((mem_len, kv.shape[1]), lambda bb, i: (bb, 0)),
                  pl.BlockSpec(wq.shape, fixed), pl.BlockSpec(wo.shape, fixed),
                  pl.BlockSpec((1, d), fixed), pl.BlockSpec((1, d), fixed),
                  pl.BlockSpec((n_e, d), fixed), pl.BlockSpec((n_e, 1), fixed)],
        out_specs=[pl.BlockSpec((tm, d), lambda bb, i: (bb * nt + i, 0)),
                   pl.BlockSpec((tm * ROW_TILES, LANE), lambda bb, i: (bb * nt + i, 0)),
                   pl.BlockSpec((TOP_K, tm), lambda bb, i: (0, bb * nt + i)),
                   pl.BlockSpec((TOP_K, tm), lambda bb, i: (0, bb * nt + i))],
        out_shape=[jax.ShapeDtypeStruct((m, d), F32),
                   jax.ShapeDtypeStruct((m * ROW_TILES, LANE), F32),
                   jax.ShapeDtypeStruct((TOP_K, m), jnp.int32),
                   jax.ShapeDtypeStruct((TOP_K, m), F32)],
        compiler_params=_params("parallel", "parallel"),
        name="cross_attn_ln",
    )(x, kv, wq, wo, g.reshape(1, d), b.reshape(1, d), wr_t, b_router.reshape(n_e, 1))


def _dispatch_tables(idx_t, n_experts):
    n_tok = idx_t.shape[1]
    n_asg = n_tok * TOP_K
    n_slots = n_asg + n_experts * MOE_SUB
    n_pass_max = n_asg // MOE_PASS + n_experts
    flat_e = idx_t.T.reshape(n_asg)
    onehot = (flat_e[:, None] == jnp.arange(n_experts, dtype=jnp.int32)[None, :]).astype(jnp.int32)
    csum = jnp.cumsum(onehot, axis=0)
    counts = csum[-1]
    rank = jnp.sum(csum * onehot, axis=1) - 1
    padded = (counts + MOE_SUB - 1) // MOE_SUB * MOE_SUB
    row_end = jnp.cumsum(padded)
    row_start = row_end - padded
    pos = (row_start[flat_e] + rank).astype(jnp.int32)
    asg = jnp.arange(n_asg, dtype=jnp.int32)
    slot_tok = jnp.zeros((n_slots,), jnp.int32).at[pos].set(asg // TOP_K)
    n_pass = (padded + MOE_PASS - 1) // MOE_PASS
    pass_end = jnp.cumsum(n_pass)
    pass_start = pass_end - n_pass
    total = pass_end[-1]
    i = jnp.arange(n_pass_max, dtype=jnp.int32)
    i_eff = jnp.minimum(i, total - 1)
    p_e = jnp.clip(jnp.searchsorted(pass_end, i_eff, side="right"), 0, n_experts - 1).astype(jnp.int32)
    p_off = (i_eff - pass_start[p_e]) * MOE_PASS
    p_row = (row_start[p_e] + p_off).astype(jnp.int32)
    p_chunks = jnp.clip((padded[p_e] - p_off) // MOE_SUB, 0, MOE_PASS // MOE_SUB)
    p_chunks = jnp.where(i < total, p_chunks, 0).astype(jnp.int32)
    used_rows = row_end[-1].astype(jnp.int32).reshape(1)
    return pos, slot_tok, used_rows, p_e, p_row, p_chunks, n_slots


def _gather_body(used_ref, tok_ref, tok_next_ref, x_hbm, o_ref, buf_ref, sem):
    j = pl.program_id(0)
    slot = j % 2

    def issue(toks, s):
        def pair(p, carry):
            for prio in range(2):
                r = 2 * p + prio
                src = pl.ds(pl.multiple_of(toks[0, r] * ROW_TILES, ROW_TILES), ROW_TILES)
                dst = pl.ds(pl.multiple_of(r * ROW_PITCH, 8), ROW_TILES)
                pltpu.make_async_copy(x_hbm.at[src], buf_ref.at[s, dst], sem.at[s]).start(priority=prio)
            return carry

        lax.fori_loop(0, MOE_SUB // 2, pair, 0, unroll=4)

    @pl.when(j == 0)
    def _():
        issue(tok_ref, 0)

    @pl.when((j + 1) * MOE_SUB < used_ref[0])
    def _():
        issue(tok_next_ref, 1 - slot)

    @pl.when(j * MOE_SUB < used_ref[0])
    def _():
        all_rows = pl.ds(0, MOE_SUB * ROW_TILES)
        pltpu.make_async_copy(x_hbm.at[all_rows], buf_ref.at[slot, all_rows], sem.at[slot]).wait()
        for c, part in enumerate(_load_row_major(buf_ref.at[slot], 0, MOE_SUB, ROW_PITCH)):
            o_ref[:, c * LANE:(c + 1) * LANE] = part.astype(o_ref.dtype)

    @pl.when(j * MOE_SUB >= used_ref[0])
    def _():
        o_ref[...] = jnp.zeros_like(o_ref)


def _dispatch_gather(x_rows, slot_tok, used_rows, n_slots):
    d = ROW_TILES * LANE
    n_blk = n_slots // MOE_SUB
    toks = slot_tok.reshape(n_blk, 1, MOE_SUB)
    return pl.pallas_call(
        _gather_body,
        grid_spec=pltpu.PrefetchScalarGridSpec(
            num_scalar_prefetch=1,
            grid=(n_blk,),
            in_specs=[pl.BlockSpec((None, 1, MOE_SUB), lambda j, used: (j, 0, 0),
                                   memory_space=pltpu.SMEM),
                      pl.BlockSpec((None, 1, MOE_SUB), lambda j, used: (jnp.minimum(j + 1, n_blk - 1), 0, 0),
                                   memory_space=pltpu.SMEM),
                      pl.BlockSpec(memory_space=pl.ANY)],
            out_specs=pl.BlockSpec((MOE_SUB, d), lambda j, used: (j, 0)),
            scratch_shapes=[pltpu.VMEM((2, MOE_SUB * ROW_PITCH, LANE), F32), pltpu.SemaphoreType.DMA((2,))],
        ),
        out_shape=jax.ShapeDtypeStruct((n_slots, d), BF16),
        compiler_params=_params("arbitrary"),
        name="moe_gather",
    )(used_rows, toks, toks, x_rows)


def _expert_body(e_ref, row_ref, nch_ref, used_ref, x_hbm, wgu_ref, wd_ref, bgu_ref, bd_ref, y_hbm,
                 x_buf, acc, gu_buf, wgu_bf, wd_perm, wd_bf, y_stage, x_sem, y_sem):
    i, j = pl.program_id(0), pl.program_id(1)
    n_items, last = pl.num_programs(0), pl.num_programs(1) - 1
    nch = nch_ref[i]
    slot = i % 2
    tf = wd_ref.shape[0]
    half = tf // 2

    def chunk(c):
        return pl.ds(pl.multiple_of(c * MOE_SUB, MOE_SUB), MOE_SUB)

    def x_copy(item, c):
        rows = pl.ds(pl.multiple_of(row_ref[item] + c * MOE_SUB, MOE_SUB), MOE_SUB)
        return pltpu.make_async_copy(x_hbm.at[rows], x_buf.at[item % 2, chunk(c)], x_sem.at[item % 2])

    def y_copy(first_row, s):
        rows = pl.ds(pl.multiple_of(first_row * ROW_TILES, MOE_SUB * ROW_TILES), MOE_SUB * ROW_TILES)
        return pltpu.make_async_copy(y_stage.at[s], y_hbm.at[rows], y_sem.at[s])

    def y_drain(n_chunks):
        for s in range(2):
            @pl.when(n_chunks > s)
            def _():
                y_copy(0, s).wait()

    def for_chunks(item, fn):
        def step(c, carry):
            fn(item, c)
            return carry

        lax.fori_loop(0, nch_ref[item], step, 0)

    @pl.when(j == 0)
    def _():
        @pl.when(i == 0)
        def _():
            for_chunks(0, lambda it, c: x_copy(it, c).start())

        @pl.when(i + 1 < n_items)
        def _():
            for_chunks(i + 1, lambda it, c: x_copy(it, c).start())

        for_chunks(i, lambda it, c: x_copy(it, c).wait())
        acc[...] = jnp.broadcast_to(bd_ref[...], acc.shape)

    even_lane = lax.broadcasted_iota(jnp.int32, (1, tf), 1) % 2 == 0

    def up_proj(c):
        return _dot(x_buf[slot, chunk(c), :], wgu_bf[...]) + bgu_ref[...]

    def down_proj(gu):
        up = jnp.clip(pltpu.roll(gu, 2 * tf - 1, 1), -SWIGLU_LIMIT, SWIGLU_LIMIT)
        gate = jnp.minimum(gu, SWIGLU_LIMIT)
        act = (up + 1.0) * (gate / (1.0 + jnp.exp(-SWIGLU_ALPHA * gate)))
        packed = jnp.where(even_lane, act[:, :tf], pltpu.roll(act[:, tf:], 1, 1))
        return _dot(packed.astype(BF16), wd_bf[...])

    def run_chunks(emit):
        wgu_bf[...] = wgu_ref[...].astype(BF16)
        for cb in range(wd_ref.shape[1] // LANE):
            cols = pl.ds(cb * LANE, LANE)
            wd_perm[cb, pl.ds(0, half, stride=2), :] = wd_ref[0:half, cols]
            wd_perm[cb, pl.ds(1, half, stride=2), :] = wd_ref[half:tf, cols]
            wd_bf[:, cols] = wd_perm[cb].astype(BF16)

        def reserve_stage(c):
            if emit:
                @pl.when(c >= 2)
                def _():
                    y_copy(0, c % 2).wait()

        def finish(c, gu):
            y = down_proj(gu)
            if emit:
                _store_row_major(y_stage.at[c % 2], acc[chunk(c), :] + y)
                y_copy(row_ref[i] + c * MOE_SUB, c % 2).start()
            else:
                acc[chunk(c), :] += y

        gu_buf[0] = up_proj(0)

        def step(c, carry):
            reserve_stage(c)
            gu = gu_buf[c % 2]
            gu_next = up_proj(c + 1)
            finish(c, gu)
            gu_buf[(c + 1) % 2] = gu_next
            return carry

        lax.fori_loop(0, nch - 1, step, 0)
        reserve_stage(nch - 1)
        finish(nch - 1, gu_buf[(nch - 1) % 2])

    @pl.when((nch > 0) & (j < last))
    def _():
        run_chunks(False)

    @pl.when(j == last)
    def _():
        @pl.when(i > 0)
        def _():
            y_drain(nch_ref[i - 1])

        @pl.when(nch > 0)
        def _():
            run_chunks(True)

        @pl.when(i == n_items - 1)
        def _():
            y_drain(nch)
            y_stage[0] = jnp.zeros(y_stage.shape[1:], y_stage.dtype)

            def fill_start(c, carry):
                y_copy(c * MOE_SUB, 0).start()
                return carry

            def fill_wait(c, carry):
                y_copy(0, 0).wait()
                return carry

            first, end = used_ref[0] // MOE_SUB, y_hbm.shape[0] // (MOE_SUB * ROW_TILES)
            lax.fori_loop(first, end, fill_start, 0)
            lax.fori_loop(first, end, fill_wait, 0)


def _experts(xb, w_gu, w_down, b_gu, b_down, layer, p_e, p_row, p_chunks, used_rows, tf):
    n_slots, d = xb.shape
    n_l, n_e, f, _ = w_down.shape
    nf = f // tf
    n_items = p_e.shape[0]

    def jeff(i, j, nch):
        return jnp.where(nch[i] > 0, j, nf - 1)

    return pl.pallas_call(
        _expert_body,
        grid_spec=pltpu.PrefetchScalarGridSpec(
            num_scalar_prefetch=4,
            grid=(n_items, nf),
            in_specs=[
                pl.BlockSpec(memory_space=pl.ANY),
                pl.BlockSpec((None, None, d, 2 * tf), lambda i, j, e, row, nch, used: (layer, e[i], 0, jeff(i, j, nch))),
                pl.BlockSpec((None, None, tf, d), lambda i, j, e, row, nch, used: (layer, e[i], jeff(i, j, nch), 0)),
                pl.BlockSpec((None, None, 1, 2 * tf), lambda i, j, e, row, nch, used: (layer, e[i], 0, jeff(i, j, nch))),
                pl.BlockSpec((None, None, 1, d), lambda i, j, e, row, nch, used: (layer, e[i], 0, 0)),
            ],
            out_specs=pl.BlockSpec(memory_space=pl.ANY),
            scratch_shapes=[pltpu.VMEM((2, MOE_PASS, d), BF16), pltpu.VMEM((MOE_PASS, d), F32),
                            pltpu.VMEM((2, MOE_SUB, 2 * tf), F32), pltpu.VMEM((d, 2 * tf), BF16),
                            pltpu.VMEM((d // LANE, tf, LANE), F32), pltpu.VMEM((tf, d), BF16),
                            pltpu.VMEM((2, MOE_SUB * ROW_TILES, LANE), F32),
                            pltpu.SemaphoreType.DMA((2,)), pltpu.SemaphoreType.DMA((2,))],
        ),
        out_shape=jax.ShapeDtypeStruct((n_slots * ROW_TILES, LANE), F32),
        compiler_params=_params("arbitrary", "arbitrary"),
        name="moe_experts",
    )(p_e, p_row, p_chunks, used_rows, xb, w_gu, w_down, b_gu.reshape(n_l, n_e, 1, 2 * f),
      b_down.reshape(n_l, n_e, 1, d))


def _combine_body(pos_ref, pos_next_ref, x_ref, gate_ref, y_hbm, g_ref, b_ref, o_ref, buf_ref, y_ref, sem,
                  *, alpha):
    tm = x_ref.shape[0]
    i = pl.program_id(0)
    slot = i % 2

    def issue(pos, s):
        for k in range(TOP_K):
            def pair(p, carry, k=k):
                for prio in range(2):
                    t = 2 * p + prio
                    src = pl.ds(pl.multiple_of(pos[k, t] * ROW_TILES, ROW_TILES), ROW_TILES)
                    dst = pl.ds(pl.multiple_of((k * tm + t) * ROW_PITCH, 8), ROW_TILES)
                    pltpu.make_async_copy(y_hbm.at[src], buf_ref.at[s, dst], sem.at[s]).start(priority=prio)
                return carry

            lax.fori_loop(0, tm // 2, pair, 0, unroll=4)

    @pl.when(i == 0)
    def _():
        issue(pos_ref, 0)

    @pl.when(i + 1 < pl.num_programs(0))
    def _():
        issue(pos_next_ref, 1 - slot)

    all_rows = pl.ds(0, TOP_K * tm * ROW_TILES)
    pltpu.make_async_copy(y_hbm.at[all_rows], buf_ref.at[slot, all_rows], sem.at[slot]).wait()
    gate = gate_ref[...]
    parts = [_load_row_major(buf_ref.at[slot], k * tm * ROW_PITCH, tm, ROW_PITCH) for k in range(TOP_K)]
    for c in range(ROW_TILES):
        y_ref[:, c * LANE:(c + 1) * LANE] = (
            (gate[:, 0:1] * parts[0][c] + gate[:, 1:2] * parts[1][c])
            + (gate[:, 2:3] * parts[2][c] + gate[:, 3:4] * parts[3][c]))
    o_ref[...] = _layer_norm_rows(alpha * x_ref[...] + y_ref[...], g_ref[...], b_ref[...])


def _combine_ln(x, yb_rows, pos, gate, g, b, alpha, tm):
    m, d = x.shape
    nt = m // tm
    pos_t = pos.reshape(nt, tm, TOP_K).transpose(0, 2, 1)
    return pl.pallas_call(
        functools.partial(_combine_body, alpha=alpha),
        grid=(nt,),
        in_specs=[pl.BlockSpec((None, TOP_K, tm), lambda i: (i, 0, 0), memory_space=pltpu.SMEM),
                  pl.BlockSpec((None, TOP_K, tm), lambda i: (jnp.minimum(i + 1, nt - 1), 0, 0),
                               memory_space=pltpu.SMEM),
                  pl.BlockSpec((tm, d), lambda i: (i, 0)),
                  pl.BlockSpec((tm, TOP_K), lambda i: (i, 0)),
                  pl.BlockSpec(memory_space=pl.ANY),
                  pl.BlockSpec((1, d), lambda i: (0, 0)), pl.BlockSpec((1, d), lambda i: (0, 0))],
        out_specs=pl.BlockSpec((tm, d), lambda i: (i, 0)),
        out_shape=jax.ShapeDtypeStruct((m, d), F32),
        scratch_shapes=[pltpu.VMEM((2, TOP_K * tm * ROW_PITCH, LANE), F32), pltpu.VMEM((tm, d), F32),
                        pltpu.SemaphoreType.DMA((2,))],
        compiler_params=_params("arbitrary"),
        name="moe_combine_ln",
    )(pos_t, pos_t, x, gate, yb_rows, g.reshape(1, d), b.reshape(1, d))


def _mixer(x, w_in, conv_w, w_gate2, b_gate2, gla_norm_g, w_out, ln_g, ln_b, alpha, bsz, seq):
    d = x.shape[1]
    ch = conv_w.shape[1]
    dk_total = w_gate2.shape[1]
    dk = dk_total // GLA_HEADS
    dv = gla_norm_g.shape[1]
    main_cols = w_in.shape[1] - GLA_GATE_RANK
    w1p = jnp.zeros((d, LANE), BF16).at[:, :GLA_GATE_RANK].set(w_in[:, main_cols:].astype(BF16))
    w2p = jnp.zeros((LANE, dk_total), BF16).at[:GLA_GATE_RANK, :].set(w_gate2.astype(BF16))
    proj, log_a = _in_proj(x, w_in[:, :main_cols].astype(BF16), w1p, w2p, b_gate2.reshape(1, dk_total),
                           tm=1024, tn=512)
    y_conv = _short_conv(proj, conv_w, bsz, seq, ch, tt=512, cw=512)
    q_off = 3 * ch
    k_off = q_off + dk_total
    v_off = k_off + dk_total
    r_off = v_off + GLA_HEADS * dv
    y_gla = _gla(proj, log_a, gla_norm_g, bsz, seq, q_off, k_off, v_off, r_off, dk, dv, tt=512, hpb=2)
    w_out_bf = w_out.astype(BF16)
    return _proj_ln(y_conv, y_gla, w_out_bf[:ch], w_out_bf[ch:], x, ln_g, ln_b, alpha, tm=512)


def _moe(x, x_rows, idx_t, gate_t, layer, w_gu, b_gu, w_down, b_down, ln_g, ln_b, alpha):
    n_e = w_gu.shape[1]
    pos, slot_tok, used_rows, p_e, p_row, p_chunks, n_slots = _dispatch_tables(idx_t, n_e)
    xb = _dispatch_gather(x_rows, slot_tok, used_rows, n_slots)
    yb = _experts(xb, w_gu, w_down, b_gu, b_down, layer, p_e, p_row, p_chunks, used_rows, tf=256)
    return _combine_ln(x, yb, pos, gate_t.T, ln_g, ln_b, alpha, tm=128)


def kernel(x, mem, w_in, conv_w, w_gate2, b_gate2, gla_norm_g, w_out, ln_mix_g, ln_mix_b, w_xq, w_xkv, w_xo, ln_xa_g, ln_xa_b, w_router, b_router, w_gu, b_gu, w_down, b_down, ln_moe_g, ln_moe_b):
    bsz, seq, d = x.shape
    depth = w_in.shape[0]
    alpha = (2 * depth) ** 0.25
    xf = x.reshape(bsz * seq, d)
    memf = mem.reshape(bsz * mem.shape[1], d)
    for l in range(depth):
        xf = _mixer(xf, w_in[l], conv_w[l], w_gate2[l], b_gate2[l], gla_norm_g[l], w_out[l],
                    ln_mix_g[l], ln_mix_b[l], alpha, bsz, seq)
        kv = _matmul(memf, w_xkv[l].astype(BF16), tm=memf.shape[0], tn=512, name="xattn_kv_proj")
        xf, x_rows, idx_t, gate_t = _cross_attn(
            xf, kv, w_xq[l].astype(BF16), w_xo[l].astype(BF16), ln_xa_g[l], ln_xa_b[l],
            w_router[l].T.astype(BF16), b_router[l], alpha, bsz, seq, tm=512)
        xf = _moe(xf, x_rows, idx_t, gate_t, l, w_gu, b_gu, w_down, b_down, ln_moe_g[l], ln_moe_b[l], alpha)
    return xf.reshape(bsz, seq, d)
```

```python
import functools

import numpy as np
import jax
import jax.numpy as jnp
from jax import lax
from jax.experimental import pallas as pl
from jax.experimental.pallas import tpu as pltpu

CONV_W = 3
GLA_HEADS = 4
GLA_GATE_RANK = 16
GLA_TAU = 16.0
XA_HEADS = 4
TOP_K = 4
SWIGLU_LIMIT = 7.0
SWIGLU_ALPHA = 1.702
LN_EPS = 1e-5
NORM_EPS = 1e-6

LANE = 128
GLA_CHUNK = 128
GLA_LEVELS = 7
MOE_SUB = 256
MOE_PASS = 1536
ROW_TILES = 16
ROW_PITCH = 24
VMEM_LIMIT = 56 * 1024 * 1024

BF16 = jnp.bfloat16
F32 = jnp.float32


def _dot(a, b):
    return jnp.dot(a, b, preferred_element_type=F32)


def _dot_nt(a, b):
    return lax.dot_general(a, b, (((1,), (1,)), ((), ())), preferred_element_type=F32)


def _dot_tn(a, b):
    return lax.dot_general(a, b, (((0,), (0,)), ((), ())), preferred_element_type=F32)


def _params(*sem):
    return pltpu.CompilerParams(dimension_semantics=sem, vmem_limit_bytes=VMEM_LIMIT)


def _store_row_major(dst_ref, val):
    n = val.shape[0]
    for c in range(ROW_TILES):
        dst_ref[pl.ds(c, n, stride=ROW_TILES), :] = val[:, c * LANE:(c + 1) * LANE]


def _load_row_major(src_ref, first, n, pitch):
    return [src_ref[pl.ds(first + c, n, stride=pitch), :] for c in range(ROW_TILES)]


def _layer_norm_rows(y, g, b):
    mu = jnp.mean(y, axis=-1, keepdims=True)
    yc = y - mu
    var = jnp.mean(yc * yc, axis=-1, keepdims=True)
    return yc * lax.rsqrt(var + LN_EPS) * g + b


def _matmul_body(a_ref, w_ref, o_ref, abf_ref):
    @pl.when(pl.program_id(1) == 0)
    def _():
        abf_ref[...] = a_ref[...].astype(BF16)

    o_ref[...] = _dot(abf_ref[...], w_ref[...]).astype(o_ref.dtype)


def _matmul(a, w, tm, tn, name):
    m, k = a.shape
    n = w.shape[1]
    return pl.pallas_call(
        _matmul_body,
        grid=(m // tm, n // tn),
        in_specs=[pl.BlockSpec((tm, k), lambda i, j: (i, 0)),
                  pl.BlockSpec((k, tn), lambda i, j: (0, j))],
        out_specs=pl.BlockSpec((tm, tn), lambda i, j: (i, j)),
        out_shape=jax.ShapeDtypeStruct((m, n), F32),
        scratch_shapes=[pltpu.VMEM((tm, k), BF16)],
        compiler_params=_params("parallel", "arbitrary"),
        name=name,
    )(a, w)


def _in_proj_body(a_ref, w_ref, w1_ref, w2_ref, b_ref, o_ref, gate_ref, abf_ref):
    @pl.when(pl.program_id(1) == 0)
    def _():
        abf_ref[...] = a_ref[...].astype(BF16)
        low = _dot(abf_ref[...], w1_ref[...])
        z = _dot(low.astype(BF16), w2_ref[...]) + b_ref[...]
        log_sig = jnp.minimum(z, 0.0) - jnp.log1p(jnp.exp(-jnp.abs(z)))
        gate_ref[...] = log_sig / GLA_TAU

    o_ref[...] = _dot(abf_ref[...], w_ref[...])


def _in_proj(a, w, w1p, w2p, b2, tm, tn):
    m, k = a.shape
    n = w.shape[1]
    ng = w2p.shape[1]
    fixed = lambda i, j: (0, 0)
    return pl.pallas_call(
        _in_proj_body,
        grid=(m // tm, n // tn),
        in_specs=[pl.BlockSpec((tm, k), lambda i, j: (i, 0)),
                  pl.BlockSpec((k, tn), lambda i, j: (0, j)),
                  pl.BlockSpec(w1p.shape, fixed), pl.BlockSpec(w2p.shape, fixed),
                  pl.BlockSpec((1, ng), fixed)],
        out_specs=[pl.BlockSpec((tm, tn), lambda i, j: (i, j)),
                   pl.BlockSpec((tm, ng), lambda i, j: (i, 0))],
        out_shape=[jax.ShapeDtypeStruct((m, n), F32), jax.ShapeDtypeStruct((m, ng), F32)],
        scratch_shapes=[pltpu.VMEM((tm, k), BF16)],
        compiler_params=_params("parallel", "arbitrary"),
        name="mixer_in_proj",
    )(a, w, w1p, w2p, b2)


def _conv_body(gc_ref, gb_ref, h_ref, gch_ref, hh_ref, w_ref, o_ref):
    u = gc_ref[...] * h_ref[...]
    uh = jnp.where(pl.program_id(1) == 0, 0.0, gch_ref[...] * hh_ref[...])
    row = lax.broadcasted_iota(jnp.int32, u.shape, 0)
    u1 = jnp.where(row == 0, uh[7:8, :], pltpu.roll(u, 1, 0))
    u2 = pltpu.roll(u, 2, 0)
    u2 = jnp.where(row == 0, uh[6:7, :], jnp.where(row == 1, uh[7:8, :], u2))
    w = w_ref[...]
    conv = w[0:1, :] * u2 + w[1:2, :] * u1 + w[2:3, :] * u
    o_ref[...] = (gb_ref[...] * conv).astype(o_ref.dtype)


def _short_conv(proj, conv_w, bsz, seq, ch, tt, cw):
    nt, nc = seq // tt, ch // cw

    def main(off):
        return pl.BlockSpec((tt, cw), lambda b, i, c: (b * nt + i, off * nc + c))

    def halo(off):
        return pl.BlockSpec(
            (8, cw), lambda b, i, c: (jnp.maximum((b * seq + i * tt) // 8 - 1, 0), off * nc + c))

    return pl.pallas_call(
        _conv_body,
        grid=(bsz, nt, nc),
        in_specs=[main(0), main(1), main(2), halo(0), halo(2),
                  pl.BlockSpec((CONV_W, cw), lambda b, i, c: (0, c))],
        out_specs=pl.BlockSpec((tt, cw), lambda b, i, c: (b * nt + i, c)),
        out_shape=jax.ShapeDtypeStruct((bsz * seq, ch), BF16),
        compiler_params=_params("parallel", "parallel", "parallel"),
        name="short_conv",
    )(proj, proj, proj, proj, proj, conv_w)


def _gla_tables():
    c = GLA_CHUNK
    i = np.arange(c)[:, None]
    t = np.arange(c)[None, :]
    mats = []
    for lvl in range(GLA_LEVELS):
        m = ((i >> (lvl + 1)) << (lvl + 1)) + (1 << lvl) - 1
        upper = (i > m) & (t > m) & (t <= i)
        lower = (i <= m) & (t > i) & (t <= m)
        mats.append(upper | lower)
    mats.append(t <= i)
    return jnp.asarray(np.concatenate(mats, axis=0).astype(np.float32), dtype=BF16)


def _gla_body(q_ref, k_ref, g_ref, v_ref, r_ref, gn_ref, tab_ref, o_ref, st_ref, *, dk_scale):
    c = GLA_CHUNK
    n_heads = st_ref.shape[0]
    dv, dk = st_ref.shape[1:]

    @pl.when(pl.program_id(2) == 0)
    def _():
        st_ref[...] = jnp.zeros_like(st_ref)

    ii = lax.broadcasted_iota(jnp.int32, (c, c), 0)
    jj = lax.broadcasted_iota(jnp.int32, (c, c), 1)
    xor = ii ^ jj
    lower = ii > jj

    for ch in range(q_ref.shape[0] // c):
        rows = pl.ds(ch * c, c)
        for hh in range(n_heads):
            kcols = pl.ds(hh * dk, dk)
            vcols = pl.ds(hh * dv, dv)
            q = q_ref[rows, kcols] * dk_scale
            k = k_ref[rows, kcols]
            g = g_ref[rows, kcols]
            v_bf = v_ref[rows, vcols].astype(BF16)
            g_hi = g.astype(BF16)
            g_lo = (g - g_hi.astype(F32)).astype(BF16)
            sums = _dot(tab_ref[...], jnp.concatenate([g_hi, g_lo], axis=1))
            sums = sums[:, :dk] + sums[:, dk:]

            scores = jnp.where(ii == jj, _dot_nt(q.astype(BF16), k.astype(BF16)), 0.0)
            for lvl in range(GLA_LEVELS):
                w = jnp.exp(sums[lvl * c:(lvl + 1) * c, :])
                s_l = _dot_nt((q * w).astype(BF16), (k * w).astype(BF16))
                scores = jnp.where(((xor >> lvl) == 1) & lower, s_l, scores)

            b = sums[GLA_LEVELS * c:, :]
            b_last = b[c - 1:c, :]
            st = st_ref[hh]
            o = _dot_nt((q * jnp.exp(b)).astype(BF16), st.astype(BF16)) + _dot(scores.astype(BF16), v_bf)
            k_dec = (k * jnp.exp(b_last - b)).astype(BF16)
            st_ref[hh] = st * jnp.exp(b_last) + _dot_tn(v_bf, k_dec)

            o = o * lax.rsqrt(jnp.mean(o * o, axis=-1, keepdims=True) + NORM_EPS) * gn_ref[hh]
            r = r_ref[rows, vcols]
            o_ref[rows, vcols] = (o * (r / (1.0 + jnp.exp(-r)))).astype(o_ref.dtype)


def _gla(proj, log_a, gn, bsz, seq, q_off, k_off, v_off, r_off, dk, dv, tt, hpb):
    nt = seq // tt
    tab = _gla_tables()

    def spec(width, off):
        return pl.BlockSpec((tt, hpb * width), lambda b, h, i: (b * nt + i, off // (hpb * width) + h))

    return pl.pallas_call(
        functools.partial(_gla_body, dk_scale=dk ** -0.5),
        grid=(bsz, GLA_HEADS // hpb, nt),
        in_specs=[spec(dk, q_off), spec(dk, k_off), spec(dk, 0), spec(dv, v_off), spec(dv, r_off),
                  pl.BlockSpec((hpb, 1, dv), lambda b, h, i: (h, 0, 0)),
                  pl.BlockSpec(tab.shape, lambda b, h, i: (0, 0))],
        out_specs=pl.BlockSpec((tt, hpb * dv), lambda b, h, i: (b * nt + i, h)),
        out_shape=jax.ShapeDtypeStruct((bsz * seq, GLA_HEADS * dv), BF16),
        scratch_shapes=[pltpu.VMEM((hpb, dv, dk), F32)],
        compiler_params=_params("parallel", "parallel", "arbitrary"),
        name="gla",
    )(proj, proj, log_a, proj, proj, gn.reshape(GLA_HEADS, 1, dv), tab)


def _proj_ln_body(a0_ref, a1_ref, w0_ref, w1_ref, x_ref, g_ref, b_ref, o_ref, *, alpha):
    y = _dot(a0_ref[...], w0_ref[...]) + _dot(a1_ref[...], w1_ref[...])
    o_ref[...] = _layer_norm_rows(alpha * x_ref[...] + y, g_ref[...], b_ref[...])


def _proj_ln(a0, a1, w0, w1, x, g, b, alpha, tm):
    m, d = x.shape
    row = lambda i: (i, 0)
    fixed = lambda i: (0, 0)
    return pl.pallas_call(
        functools.partial(_proj_ln_body, alpha=alpha),
        grid=(m // tm,),
        in_specs=[pl.BlockSpec((tm, a0.shape[1]), row), pl.BlockSpec((tm, a1.shape[1]), row),
                  pl.BlockSpec(w0.shape, fixed), pl.BlockSpec(w1.shape, fixed),
                  pl.BlockSpec((tm, d), row), pl.BlockSpec((1, d), fixed), pl.BlockSpec((1, d), fixed)],
        out_specs=pl.BlockSpec((tm, d), row),
        out_shape=jax.ShapeDtypeStruct((m, d), F32),
        compiler_params=_params("parallel"),
        name="mixer_out_ln",
    )(a0, a1, w0, w1, x, g.reshape(1, d), b.reshape(1, d))


def _route(x, wt_ref, b_ref, idx_ref, gate_ref):
    logits = _dot_nt(wt_ref[...], x.astype(BF16)) + b_ref[...]
    n_e = logits.shape[0]
    e_iota = lax.broadcasted_iota(jnp.int32, logits.shape, 0)
    vals, idxs = [], []
    for _ in range(TOP_K):
        best = jnp.max(logits, axis=0, keepdims=True)
        sel = jnp.min(jnp.where(logits == best, e_iota, n_e), axis=0, keepdims=True)
        vals.append(best)
        idxs.append(sel)
        logits = jnp.where(e_iota == sel, -jnp.inf, logits)
    exps = [jnp.exp(v - vals[0]) for v in vals]
    total = exps[0] + exps[1] + exps[2] + exps[3]
    idx_ref[...] = jnp.concatenate(idxs, axis=0)
    gate_ref[...] = jnp.concatenate([e / total for e in exps], axis=0)


def _xattn_body(x_ref, kv_ref, wq_ref, wo_ref, g_ref, b_ref, wr_ref, br_ref, o_ref, rows_ref, idx_ref, gate_ref,
                *, alpha):
    x = x_ref[...]
    xa = wq_ref.shape[1]
    hd = xa // XA_HEADS
    q = _dot(x.astype(BF16), wq_ref[...])
    heads = []
    for h in range(XA_HEADS):
        qh = q[:, h * hd:(h + 1) * hd].astype(BF16)
        kh = kv_ref[:, h * hd:(h + 1) * hd].astype(BF16)
        vh = kv_ref[:, xa + h * hd:xa + (h + 1) * hd].astype(BF16)
        s = _dot_nt(qh, kh) * hd ** -0.5
        p = jnp.exp(s - jnp.max(s, axis=-1, keepdims=True))
        p = p / jnp.sum(p, axis=-1, keepdims=True)
        heads.append(_dot(p.astype(BF16), vh))
    o = jnp.concatenate(heads, axis=1).astype(BF16)
    y = _dot(o, wo_ref[...])
    out = _layer_norm_rows(alpha * x + y, g_ref[...], b_ref[...])
    o_ref[...] = out
    _store_row_major(rows_ref, out)
    _route(out, wr_ref, br_ref, idx_ref, gate_ref)


def _cross_attn(x, kv, wq, wo, g, b, wr_t, b_router, alpha, bsz, seq, tm):
    m, d = x.shape
    n_e = wr_t.shape[0]
    nt = seq // tm
    mem_len = kv.shape[0] // bsz
    fixed = lambda bb, i: (0, 0)
    return pl.pallas_call(
        functools.partial(_xattn_body, alpha=alpha),
        grid=(bsz, nt),
        in_specs=[pl.BlockSpec((tm, d), lambda bb, i: (bb * nt + i, 0)),
                  pl.BlockSpec((mem_len, kv.shape[1]), lambda bb, i: (bb, 0)),
                  pl.BlockSpec(wq.shape, fixed), pl.BlockSpec(wo.shape, fixed),
                  pl.BlockSpec((1, d), fixed), pl.BlockSpec((1, d), fixed),
                  pl.BlockSpec((n_e, d), fixed), pl.BlockSpec((n_e, 1), fixed)],
        out_specs=[pl.BlockSpec((tm, d), lambda bb, i: (bb * nt + i, 0)),
                   pl.BlockSpec((tm * ROW_TILES, LANE), lambda bb, i: (bb * nt + i, 0)),
                   pl.BlockSpec((TOP_K, tm), lambda bb, i: (0, bb * nt + i)),
                   pl.BlockSpec((TOP_K, tm), lambda bb, i: (0, bb * nt + i))],
        out_shape=[jax.ShapeDtypeStruct((m, d), F32),
                   jax.ShapeDtypeStruct((m * ROW_TILES, LANE), F32),
                   jax.ShapeDtypeStruct((TOP_K, m), jnp.int32),
                   jax.ShapeDtypeStruct((TOP_K, m), F32)],
        compiler_params=_params("parallel", "parallel"),
        name="cross_attn_ln",
    )(x, kv, wq, wo, g.reshape(1, d), b.reshape(1, d), wr_t, b_router.reshape(n_e, 1))


def _dispatch_tables(idx_t, n_experts):
    n_tok = idx_t.shape[1]
    n_asg = n_tok * TOP_K
    n_slots = n_asg + n_experts * MOE_SUB
    n_pass_max = n_asg // MOE_PASS + n_experts
    flat_e = idx_t.T.reshape(n_asg)
    onehot = (flat_e[:, None] == jnp.arange(n_experts, dtype=jnp.int32)[None, :]).astype(jnp.int32)
    csum = jnp.cumsum(onehot, axis=0)
    counts = csum[-1]
    rank = jnp.sum(csum * onehot, axis=1) - 1
    padded = (counts + MOE_SUB - 1) // MOE_SUB * MOE_SUB
    row_end = jnp.cumsum(padded)
    row_start = row_end - padded
    pos = (row_start[flat_e] + rank).astype(jnp.int32)
    asg = jnp.arange(n_asg, dtype=jnp.int32)
    slot_tok = jnp.zeros((n_slots,), jnp.int32).at[pos].set(asg // TOP_K)
    n_pass = (padded + MOE_PASS - 1) // MOE_PASS
    pass_end = jnp.cumsum(n_pass)
    pass_start = pass_end - n_pass
    total = pass_end[-1]
    i = jnp.arange(n_pass_max, dtype=jnp.int32)
    i_eff = jnp.minimum(i, total - 1)
    p_e = jnp.clip(jnp.searchsorted(pass_end, i_eff, side="right"), 0, n_experts - 1).astype(jnp.int32)
    p_off = (i_eff - pass_start[p_e]) * MOE_PASS
    p_row = (row_start[p_e] + p_off).astype(jnp.int32)
    p_chunks = jnp.clip((padded[p_e] - p_off) // MOE_SUB, 0, MOE_PASS // MOE_SUB)
    p_chunks = jnp.where(i < total, p_chunks, 0).astype(jnp.int32)
    used_rows = row_end[-1].astype(jnp.int32).reshape(1)
    return pos, slot_tok, used_rows, p_e, p_row, p_chunks, n_slots


def _gather_body(used_ref, tok_ref, tok_next_ref, x_hbm, o_ref, buf_ref, sem):
    j = pl.program_id(0)
    slot = j % 2

    def issue(toks, s):
        def pair(p, carry):
            for prio in range(2):
                r = 2 * p + prio
                src = pl.ds(pl.multiple_of(toks[0, r] * ROW_TILES, ROW_TILES), ROW_TILES)
                dst = pl.ds(pl.multiple_of(r * ROW_PITCH, 8), ROW_TILES)
                pltpu.make_async_copy(x_hbm.at[src], buf_ref.at[s, dst], sem.at[s]).start(priority=prio)
            return carry

        lax.fori_loop(0, MOE_SUB // 2, pair, 0, unroll=4)

    @pl.when(j == 0)
    def _():
        issue(tok_ref, 0)

    @pl.when((j + 1) * MOE_SUB < used_ref[0])
    def _():
        issue(tok_next_ref, 1 - slot)

    @pl.when(j * MOE_SUB < used_ref[0])
    def _():
        all_rows = pl.ds(0, MOE_SUB * ROW_TILES)
        pltpu.make_async_copy(x_hbm.at[all_rows], buf_ref.at[slot, all_rows], sem.at[slot]).wait()
        for c, part in enumerate(_load_row_major(buf_ref.at[slot], 0, MOE_SUB, ROW_PITCH)):
            o_ref[:, c * LANE:(c + 1) * LANE] = part.astype(o_ref.dtype)

    @pl.when(j * MOE_SUB >= used_ref[0])
    def _():
        o_ref[...] = jnp.zeros_like(o_ref)


def _dispatch_gather(x_rows, slot_tok, used_rows, n_slots):
    d = ROW_TILES * LANE
    n_blk = n_slots // MOE_SUB
    toks = slot_tok.reshape(n_blk, 1, MOE_SUB)
    return pl.pallas_call(
        _gather_body,
        grid_spec=pltpu.PrefetchScalarGridSpec(
            num_scalar_prefetch=1,
            grid=(n_blk,),
            in_specs=[pl.BlockSpec((None, 1, MOE_SUB), lambda j, used: (j, 0, 0),
                                   memory_space=pltpu.SMEM),
                      pl.BlockSpec((None, 1, MOE_SUB), lambda j, used: (jnp.minimum(j + 1, n_blk - 1), 0, 0),
                                   memory_space=pltpu.SMEM),
                      pl.BlockSpec(memory_space=pl.ANY)],
            out_specs=pl.BlockSpec((MOE_SUB, d), lambda j, used: (j, 0)),
            scratch_shapes=[pltpu.VMEM((2, MOE_SUB * ROW_PITCH, LANE), F32), pltpu.SemaphoreType.DMA((2,))],
        ),
        out_shape=jax.ShapeDtypeStruct((n_slots, d), BF16),
        compiler_params=_params("arbitrary"),
        name="moe_gather",
    )(used_rows, toks, toks, x_rows)


def _expert_body(e_ref, row_ref, nch_ref, used_ref, x_hbm, wgu_ref, wd_ref, bgu_ref, bd_ref, y_hbm,
                 x_buf, acc, gu_buf, wgu_bf, wd_perm, wd_bf, y_stage, x_sem, y_sem):
    i, j = pl.program_id(0), pl.program_id(1)
    n_items, last = pl.num_programs(0), pl.num_programs(1) - 1
    nch = nch_ref[i]
    slot = i % 2
    tf = wd_ref.shape[0]
    half = tf // 2

    def chunk(c):
        return pl.ds(pl.multiple_of(c * MOE_SUB, MOE_SUB), MOE_SUB)

    def x_copy(item, c):
        rows = pl.ds(pl.multiple_of(row_ref[item] + c * MOE_SUB, MOE_SUB), MOE_SUB)
        return pltpu.make_async_copy(x_hbm.at[rows], x_buf.at[item % 2, chunk(c)], x_sem.at[item % 2])

    def y_copy(first_row, s):
        rows = pl.ds(pl.multiple_of(first_row * ROW_TILES, MOE_SUB * ROW_TILES), MOE_SUB * ROW_TILES)
        return pltpu.make_async_copy(y_stage.at[s], y_hbm.at[rows], y_sem.at[s])

    def y_drain(n_chunks):
        for s in range(2):
            @pl.when(n_chunks > s)
            def _():
                y_copy(0, s).wait()

    def for_chunks(item, fn):
        def step(c, carry):
            fn(item, c)
            return carry

        lax.fori_loop(0, nch_ref[item], step, 0)

    @pl.when(j == 0)
    def _():
        @pl.when(i == 0)
        def _():
            for_chunks(0, lambda it, c: x_copy(it, c).start())

        @pl.when(i + 1 < n_items)
        def _():
            for_chunks(i + 1, lambda it, c: x_copy(it, c).start())

        for_chunks(i, lambda it, c: x_copy(it, c).wait())
        acc[...] = jnp.broadcast_to(bd_ref[...], acc.shape)

    even_lane = lax.broadcasted_iota(jnp.int32, (1, tf), 1) % 2 == 0

    def up_proj(c):
        return _dot(x_buf[slot, chunk(c), :], wgu_bf[...]) + bgu_ref[...]

    def down_proj(gu, wd):
        up = jnp.clip(pltpu.roll(gu, 2 * tf - 1, 1), -SWIGLU_LIMIT, SWIGLU_LIMIT)
        gate = jnp.minimum(gu, SWIGLU_LIMIT)
        act = (up + 1.0) * (gate / (1.0 + jnp.exp(-SWIGLU_ALPHA * gate)))
        packed = jnp.where(even_lane, act[:, :tf], pltpu.roll(act[:, tf:], 1, 1))
        return _dot(packed.astype(BF16), wd)

    wd_cur = j % 2

    def prepare_weights():
        wgu_bf[...] = wgu_ref[...].astype(BF16)
        for cb in range(wd_ref.shape[1] // LANE):
            cols = pl.ds(cb * LANE, LANE)
            wd_perm[cb, pl.ds(0, half, stride=2), :] = wd_ref[0:half, cols]
            wd_perm[cb, pl.ds(1, half, stride=2), :] = wd_ref[half:tf, cols]
            wd_bf[wd_cur, :, cols] = wd_perm[cb].astype(BF16)

    def carried_trip(k, carry):
        first = k == 0
        c_fin = jnp.where(first, nch - 1, k - 1)
        gu = gu_buf[c_fin % 2]
        gu_next = up_proj(k)
        acc[chunk(c_fin), :] += down_proj(gu, wd_bf[jnp.where(first, 1 - wd_cur, wd_cur)])
        gu_buf[k % 2] = gu_next
        return carry

    def plain_trip(k, carry, emit):
        c_fin = k - 1
        if emit:
            @pl.when(c_fin >= 2)
            def _():
                y_copy(0, c_fin % 2).wait()
        gu = gu_buf[c_fin % 2]
        gu_next = up_proj(k)
        y = down_proj(gu, wd_bf[wd_cur])
        if emit:
            _store_row_major(y_stage.at[c_fin % 2], acc[chunk(c_fin), :] + y)
            y_copy(row_ref[i] + c_fin * MOE_SUB, c_fin % 2).start()
        else:
            acc[chunk(c_fin), :] += y
        gu_buf[k % 2] = gu_next
        return carry

    @pl.when((nch > 0) & (j == 0))
    def _():
        prepare_weights()
        gu_buf[0] = up_proj(0)
        lax.fori_loop(1, nch, functools.partial(plain_trip, emit=False), 0)

    @pl.when((nch > 0) & (j > 0) & (j < last))
    def _():
        prepare_weights()
        lax.fori_loop(0, nch, carried_trip, 0)

    @pl.when(j == last)
    def _():
        @pl.when(i > 0)
        def _():
            y_drain(nch_ref[i - 1])

        @pl.when(nch > 0)
        def _():
            prepare_weights()
            carried_trip(jnp.int32(0), 0)
            lax.fori_loop(1, nch, functools.partial(plain_trip, emit=True), 0)
            c_fin = nch - 1

            @pl.when(c_fin >= 2)
            def _():
                y_copy(0, c_fin % 2).wait()

            y = down_proj(gu_buf[c_fin % 2], wd_bf[wd_cur])
            _store_row_major(y_stage.at[c_fin % 2], acc[chunk(c_fin), :] + y)
            y_copy(row_ref[i] + c_fin * MOE_SUB, c_fin % 2).start()

        @pl.when(i == n_items - 1)
        def _():
            y_drain(nch)
            y_stage[0] = jnp.zeros(y_stage.shape[1:], y_stage.dtype)

            def fill_start(c, carry):
                y_copy(c * MOE_SUB, 0).start()
                return carry

            def fill_wait(c, carry):
                y_copy(0, 0).wait()
                return carry

            first, end = used_ref[0] // MOE_SUB, y_hbm.shape[0] // (MOE_SUB * ROW_TILES)
            lax.fori_loop(first, end, fill_start, 0)
            lax.fori_loop(first, end, fill_wait, 0)


def _experts(xb, w_gu, w_down, b_gu, b_down, layer, p_e, p_row, p_chunks, used_rows, tf):
    n_slots, d = xb.shape
    n_l, n_e, f, _ = w_down.shape
    nf = f // tf
    assert nf >= 2
    n_items = p_e.shape[0]

    def jeff(i, j, nch):
        return jnp.where(nch[i] > 0, j, nf - 1)

    return pl.pallas_call(
        _expert_body,
        grid_spec=pltpu.PrefetchScalarGridSpec(
            num_scalar_prefetch=4,
            grid=(n_items, nf),
            in_specs=[
                pl.BlockSpec(memory_space=pl.ANY),
                pl.BlockSpec((None, None, d, 2 * tf), lambda i, j, e, row, nch, used: (layer, e[i], 0, jeff(i, j, nch))),
                pl.BlockSpec((None, None, tf, d), lambda i, j, e, row, nch, used: (layer, e[i], jeff(i, j, nch), 0)),
                pl.BlockSpec((None, None, 1, 2 * tf), lambda i, j, e, row, nch, used: (layer, e[i], 0, jeff(i, j, nch))),
                pl.BlockSpec((None, None, 1, d), lambda i, j, e, row, nch, used: (layer, e[i], 0, 0)),
            ],
            out_specs=pl.BlockSpec(memory_space=pl.ANY),
            scratch_shapes=[pltpu.VMEM((2, MOE_PASS, d), BF16), pltpu.VMEM((MOE_PASS, d), F32),
                            pltpu.VMEM((2, MOE_SUB, 2 * tf), F32), pltpu.VMEM((d, 2 * tf), BF16),
                            pltpu.VMEM((d // LANE, tf, LANE), F32), pltpu.VMEM((2, tf, d), BF16),
                            pltpu.VMEM((2, MOE_SUB * ROW_TILES, LANE), F32),
                            pltpu.SemaphoreType.DMA((2,)), pltpu.SemaphoreType.DMA((2,))],
        ),
        out_shape=jax.ShapeDtypeStruct((n_slots * ROW_TILES, LANE), F32),
        compiler_params=_params("arbitrary", "arbitrary"),
        name="moe_experts",
    )(p_e, p_row, p_chunks, used_rows, xb, w_gu, w_down, b_gu.reshape(n_l, n_e, 1, 2 * f),
      b_down.reshape(n_l, n_e, 1, d))


def _combine_body(pos_ref, pos_next_ref, x_ref, gate_ref, y_hbm, g_ref, b_ref, o_ref, buf_ref, y_ref, sem,
                  *, alpha):
    tm = x_ref.shape[0]
    i = pl.program_id(0)
    slot = i % 2

    def issue(pos, s):
        for k in range(TOP_K):
            def pair(p, carry, k=k):
                for prio in range(2):
                    t = 2 * p + prio
                    src = pl.ds(pl.multiple_of(pos[k, t] * ROW_TILES, ROW_TILES), ROW_TILES)
                    dst = pl.ds(pl.multiple_of((k * tm + t) * ROW_PITCH, 8), ROW_TILES)
                    pltpu.make_async_copy(y_hbm.at[src], buf_ref.at[s, dst], sem.at[s]).start(priority=prio)
                return carry

            lax.fori_loop(0, tm // 2, pair, 0, unroll=4)

    @pl.when(i == 0)
    def _():
        issue(pos_ref, 0)

    @pl.when(i + 1 < pl.num_programs(0))
    def _():
        issue(pos_next_ref, 1 - slot)

    all_rows = pl.ds(0, TOP_K * tm * ROW_TILES)
    pltpu.make_async_copy(y_hbm.at[all_rows], buf_ref.at[slot, all_rows], sem.at[slot]).wait()
    gate = gate_ref[...]
    parts = [_load_row_major(buf_ref.at[slot], k * tm * ROW_PITCH, tm, ROW_PITCH) for k in range(TOP_K)]
    for c in range(ROW_TILES):
        y_ref[:, c * LANE:(c + 1) * LANE] = (
            (gate[:, 0:1] * parts[0][c] + gate[:, 1:2] * parts[1][c])
            + (gate[:, 2:3] * parts[2][c] + gate[:, 3:4] * parts[3][c]))
    o_ref[...] = _layer_norm_rows(alpha * x_ref[...] + y_ref[...], g_ref[...], b_ref[...])


def _combine_ln(x, yb_rows, pos, gate, g, b, alpha, tm):
    m, d = x.shape
    nt = m // tm
    pos_t = pos.reshape(nt, tm, TOP_K).transpose(0, 2, 1)
    return pl.pallas_call(
        functools.partial(_combine_body, alpha=alpha),
        grid=(nt,),
        in_specs=[pl.BlockSpec((None, TOP_K, tm), lambda i: (i, 0, 0), memory_space=pltpu.SMEM),
                  pl.BlockSpec((None, TOP_K, tm), lambda i: (jnp.minimum(i + 1, nt - 1), 0, 0),
                               memory_space=pltpu.SMEM),
                  pl.BlockSpec((tm, d), lambda i: (i, 0)),
                  pl.BlockSpec((tm, TOP_K), lambda i: (i, 0)),
                  pl.BlockSpec(memory_space=pl.ANY),
                  pl.BlockSpec((1, d), lambda i: (0, 0)), pl.BlockSpec((1, d), lambda i: (0, 0))],
        out_specs=pl.BlockSpec((tm, d), lambda i: (i, 0)),
        out_shape=jax.ShapeDtypeStruct((m, d), F32),
        scratch_shapes=[pltpu.VMEM((2, TOP_K * tm * ROW_PITCH, LANE), F32), pltpu.VMEM((tm, d), F32),
                        pltpu.SemaphoreType.DMA((2,))],
        compiler_params=_params("arbitrary"),
        name="moe_combine_ln",
    )(pos_t, pos_t, x, gate, yb_rows, g.reshape(1, d), b.reshape(1, d))


def _mixer(x, w_in, conv_w, w_gate2, b_gate2, gla_norm_g, w_out, ln_g, ln_b, alpha, bsz, seq):
    d = x.shape[1]
    ch = conv_w.shape[1]
    dk_total = w_gate2.shape[1]
    dk = dk_total // GLA_HEADS
    dv = gla_norm_g.shape[1]
    main_cols = w_in.shape[1] - GLA_GATE_RANK
    w1p = jnp.zeros((d, LANE), BF16).at[:, :GLA_GATE_RANK].set(w_in[:, main_cols:].astype(BF16))
    w2p = jnp.zeros((LANE, dk_total), BF16).at[:GLA_GATE_RANK, :].set(w_gate2.astype(BF16))
    proj, log_a = _in_proj(x, w_in[:, :main_cols].astype(BF16), w1p, w2p, b_gate2.reshape(1, dk_total),
                           tm=1024, tn=512)
    y_conv = _short_conv(proj, conv_w, bsz, seq, ch, tt=512, cw=512)
    q_off = 3 * ch
    k_off = q_off + dk_total
    v_off = k_off + dk_total
    r_off = v_off + GLA_HEADS * dv
    y_gla = _gla(proj, log_a, gla_norm_g, bsz, seq, q_off, k_off, v_off, r_off, dk, dv, tt=512, hpb=2)
    w_out_bf = w_out.astype(BF16)
    return _proj_ln(y_conv, y_gla, w_out_bf[:ch], w_out_bf[ch:], x, ln_g, ln_b, alpha, tm=512)


def _moe(x, x_rows, idx_t, gate_t, layer, w_gu, b_gu, w_down, b_down, ln_g, ln_b, alpha):
    n_e = w_gu.shape[1]
    pos, slot_tok, used_rows, p_e, p_row, p_chunks, n_slots = _dispatch_tables(idx_t, n_e)
    xb = _dispatch_gather(x_rows, slot_tok, used_rows, n_slots)
    yb = _experts(xb, w_gu, w_down, b_gu, b_down, layer, p_e, p_row, p_chunks, used_rows, tf=256)
    return _combine_ln(x, yb, pos, gate_t.T, ln_g, ln_b, alpha, tm=128)


def kernel(x, mem, w_in, conv_w, w_gate2, b_gate2, gla_norm_g, w_out, ln_mix_g, ln_mix_b, w_xq, w_xkv, w_xo, ln_xa_g, ln_xa_b, w_router, b_router, w_gu, b_gu, w_down, b_down, ln_moe_g, ln_moe_b):
    bsz, seq, d = x.shape
    depth = w_in.shape[0]
    alpha = (2 * depth) ** 0.25
    xf = x.reshape(bsz * seq, d)
    memf = mem.reshape(bsz * mem.shape[1], d)
    for l in range(depth):
        xf = _mixer(xf, w_in[l], conv_w[l], w_gate2[l], b_gate2[l], gla_norm_g[l], w_out[l],
                    ln_mix_g[l], ln_mix_b[l], alpha, bsz, seq)
        kv = _matmul(memf, w_xkv[l].astype(BF16), tm=memf.shape[0], tn=512, name="xattn_kv_proj")
        xf, x_rows, idx_t, gate_t = _cross_attn(
            xf, kv, w_xq[l].astype(BF16), w_xo[l].astype(BF16), ln_xa_g[l], ln_xa_b[l],
            w_router[l].T.astype(BF16), b_router[l], alpha, bsz, seq, tm=512)
        xf = _moe(xf, x_rows, idx_t, gate_t, l, w_gu, b_gu, w_down, b_down, ln_moe_g[l], ln_moe_b[l], alpha)
    return xf.reshape(bsz, seq, d)
```

```python
import functools

import numpy as np
import jax
import jax.numpy as jnp
from jax import lax
from jax.experimental import pallas as pl
from jax.experimental.pallas import tpu as pltpu

CONV_W = 3
GLA_HEADS = 4
GLA_GATE_RANK = 16
GLA_TAU = 16.0
XA_HEADS = 4
TOP_K = 4
SWIGLU_LIMIT = 7.0
SWIGLU_ALPHA = 1.702
LN_EPS = 1e-5
NORM_EPS = 1e-6

LANE = 128
GLA_CHUNK = 128
GLA_LEVELS = 7
MOE_SUB = 272
MOE_PASS = 6 * MOE_SUB
ROW_TILES = 16
ROW_PITCH = 24
VMEM_LIMIT = 56 * 1024 * 1024

BF16 = jnp.bfloat16
F32 = jnp.float32


def _dot(a, b):
    return jnp.dot(a, b, preferred_element_type=F32)


def _dot_nt(a, b):
    return lax.dot_general(a, b, (((1,), (1,)), ((), ())), preferred_element_type=F32)


def _dot_tn(a, b):
    return lax.dot_general(a, b, (((0,), (0,)), ((), ())), preferred_element_type=F32)


def _params(*sem):
    return pltpu.CompilerParams(dimension_semantics=sem, vmem_limit_bytes=VMEM_LIMIT)


def _store_row_major(dst_ref, val):
    n = val.shape[0]
    for c in range(ROW_TILES):
        dst_ref[pl.ds(c, n, stride=ROW_TILES), :] = val[:, c * LANE:(c + 1) * LANE]


def _load_row_major(src_ref, first, n, pitch):
    return [src_ref[pl.ds(first + c, n, stride=pitch), :] for c in range(ROW_TILES)]


def _layer_norm_rows(y, g, b):
    mu = jnp.mean(y, axis=-1, keepdims=True)
    yc = y - mu
    var = jnp.mean(yc * yc, axis=-1, keepdims=True)
    return yc * lax.rsqrt(var + LN_EPS) * g + b


def _matmul_body(a_ref, w_ref, o_ref, abf_ref):
    @pl.when(pl.program_id(1) == 0)
    def _():
        abf_ref[...] = a_ref[...].astype(BF16)

    o_ref[...] = _dot(abf_ref[...], w_ref[...]).astype(o_ref.dtype)


def _matmul(a, w, tm, tn, name):
    m, k = a.shape
    n = w.shape[1]
    return pl.pallas_call(
        _matmul_body,
        grid=(m // tm, n // tn),
        in_specs=[pl.BlockSpec((tm, k), lambda i, j: (i, 0)),
                  pl.BlockSpec((k, tn), lambda i, j: (0, j))],
        out_specs=pl.BlockSpec((tm, tn), lambda i, j: (i, j)),
        out_shape=jax.ShapeDtypeStruct((m, n), F32),
        scratch_shapes=[pltpu.VMEM((tm, k), BF16)],
        compiler_params=_params("parallel", "arbitrary"),
        name=name,
    )(a, w)


def _in_proj_body(a_ref, w_ref, w1_ref, w2_ref, b_ref, o_ref, gate_ref, abf_ref):
    @pl.when(pl.program_id(1) == 0)
    def _():
        abf_ref[...] = a_ref[...].astype(BF16)
        low = _dot(abf_ref[...], w1_ref[...])
        z = _dot(low.astype(BF16), w2_ref[...]) + b_ref[...]
        log_sig = jnp.minimum(z, 0.0) - jnp.log1p(jnp.exp(-jnp.abs(z)))
        gate_ref[...] = log_sig / GLA_TAU

    o_ref[...] = _dot(abf_ref[...], w_ref[...])


def _in_proj(a, w, w1p, w2p, b2, tm, tn):
    m, k = a.shape
    n = w.shape[1]
    ng = w2p.shape[1]
    fixed = lambda i, j: (0, 0)
    return pl.pallas_call(
        _in_proj_body,
        grid=(m // tm, n // tn),
        in_specs=[pl.BlockSpec((tm, k), lambda i, j: (i, 0)),
                  pl.BlockSpec((k, tn), lambda i, j: (0, j)),
                  pl.BlockSpec(w1p.shape, fixed), pl.BlockSpec(w2p.shape, fixed),
                  pl.BlockSpec((1, ng), fixed)],
        out_specs=[pl.BlockSpec((tm, tn), lambda i, j: (i, j)),
                   pl.BlockSpec((tm, ng), lambda i, j: (i, 0))],
        out_shape=[jax.ShapeDtypeStruct((m, n), F32), jax.ShapeDtypeStruct((m, ng), F32)],
        scratch_shapes=[pltpu.VMEM((tm, k), BF16)],
        compiler_params=_params("parallel", "arbitrary"),
        name="mixer_in_proj",
    )(a, w, w1p, w2p, b2)


def _conv_body(gc_ref, gb_ref, h_ref, gch_ref, hh_ref, w_ref, o_ref):
    u = gc_ref[...] * h_ref[...]
    uh = jnp.where(pl.program_id(1) == 0, 0.0, gch_ref[...] * hh_ref[...])
    row = lax.broadcasted_iota(jnp.int32, u.shape, 0)
    u1 = jnp.where(row == 0, uh[7:8, :], pltpu.roll(u, 1, 0))
    u2 = pltpu.roll(u, 2, 0)
    u2 = jnp.where(row == 0, uh[6:7, :], jnp.where(row == 1, uh[7:8, :], u2))
    w = w_ref[...]
    conv = w[0:1, :] * u2 + w[1:2, :] * u1 + w[2:3, :] * u
    o_ref[...] = (gb_ref[...] * conv).astype(o_ref.dtype)


def _short_conv(proj, conv_w, bsz, seq, ch, tt, cw):
    nt, nc = seq // tt, ch // cw

    def main(off):
        return pl.BlockSpec((tt, cw), lambda b, i, c: (b * nt + i, off * nc + c))

    def halo(off):
        return pl.BlockSpec(
            (8, cw), lambda b, i, c: (jnp.maximum((b * seq + i * tt) // 8 - 1, 0), off * nc + c))

    return pl.pallas_call(
        _conv_body,
        grid=(bsz, nt, nc),
        in_specs=[main(0), main(1), main(2), halo(0), halo(2),
                  pl.BlockSpec((CONV_W, cw), lambda b, i, c: (0, c))],
        out_specs=pl.BlockSpec((tt, cw), lambda b, i, c: (b * nt + i, c)),
        out_shape=jax.ShapeDtypeStruct((bsz * seq, ch), BF16),
        compiler_params=_params("parallel", "parallel", "parallel"),
        name="short_conv",
    )(proj, proj, proj, proj, proj, conv_w)


def _gla_tables():
    c = GLA_CHUNK
    i = np.arange(c)[:, None]
    t = np.arange(c)[None, :]
    mats = []
    for lvl in range(GLA_LEVELS):
        m = ((i >> (lvl + 1)) << (lvl + 1)) + (1 << lvl) - 1
        upper = (i > m) & (t > m) & (t <= i)
        lower = (i <= m) & (t > i) & (t <= m)
        mats.append(upper | lower)
    mats.append(t <= i)
    return jnp.asarray(np.concatenate(mats, axis=0).astype(np.float32), dtype=BF16)


def _gla_body(q_ref, k_ref, g_ref, v_ref, r_ref, gn_ref, tab_ref, o_ref, st_ref, *, dk_scale):
    c = GLA_CHUNK
    n_heads = st_ref.shape[0]
    dv, dk = st_ref.shape[1:]

    @pl.when(pl.program_id(2) == 0)
    def _():
        st_ref[...] = jnp.zeros_like(st_ref)

    ii = lax.broadcasted_iota(jnp.int32, (c, c), 0)
    jj = lax.broadcasted_iota(jnp.int32, (c, c), 1)
    xor = ii ^ jj
    lower = ii > jj

    for ch in range(q_ref.shape[0] // c):
        rows = pl.ds(ch * c, c)
        for hh in range(n_heads):
            kcols = pl.ds(hh * dk, dk)
            vcols = pl.ds(hh * dv, dv)
            q = q_ref[rows, kcols] * dk_scale
            k = k_ref[rows, kcols]
            g = g_ref[rows, kcols]
            v_bf = v_ref[rows, vcols].astype(BF16)
            g_hi = g.astype(BF16)
            g_lo = (g - g_hi.astype(F32)).astype(BF16)
            sums = _dot(tab_ref[...], jnp.concatenate([g_hi, g_lo], axis=1))
            sums = sums[:, :dk] + sums[:, dk:]

            scores = jnp.where(ii == jj, _dot_nt(q.astype(BF16), k.astype(BF16)), 0.0)
            for lvl in range(GLA_LEVELS):
                w = jnp.exp(sums[lvl * c:(lvl + 1) * c, :])
                s_l = _dot_nt((q * w).astype(BF16), (k * w).astype(BF16))
                scores = jnp.where(((xor >> lvl) == 1) & lower, s_l, scores)

            b = sums[GLA_LEVELS * c:, :]
            b_last = b[c - 1:c, :]
            st = st_ref[hh]
            o = _dot_nt((q * jnp.exp(b)).astype(BF16), st.astype(BF16)) + _dot(scores.astype(BF16), v_bf)
            k_dec = (k * jnp.exp(b_last - b)).astype(BF16)
            st_ref[hh] = st * jnp.exp(b_last) + _dot_tn(v_bf, k_dec)

            o = o * lax.rsqrt(jnp.mean(o * o, axis=-1, keepdims=True) + NORM_EPS) * gn_ref[hh]
            r = r_ref[rows, vcols]
            o_ref[rows, vcols] = (o * (r / (1.0 + jnp.exp(-r)))).astype(o_ref.dtype)


def _gla(proj, log_a, gn, bsz, seq, q_off, k_off, v_off, r_off, dk, dv, tt, hpb):
    nt = seq // tt
    tab = _gla_tables()

    def spec(width, off):
        return pl.BlockSpec((tt, hpb * width), lambda b, h, i: (b * nt + i, off // (hpb * width) + h))

    return pl.pallas_call(
        functools.partial(_gla_body, dk_scale=dk ** -0.5),
        grid=(bsz, GLA_HEADS // hpb, nt),
        in_specs=[spec(dk, q_off), spec(dk, k_off), spec(dk, 0), spec(dv, v_off), spec(dv, r_off),
                  pl.BlockSpec((hpb, 1, dv), lambda b, h, i: (h, 0, 0)),
                  pl.BlockSpec(tab.shape, lambda b, h, i: (0, 0))],
        out_specs=pl.BlockSpec((tt, hpb * dv), lambda b, h, i: (b * nt + i, h)),
        out_shape=jax.ShapeDtypeStruct((bsz * seq, GLA_HEADS * dv), BF16),
        scratch_shapes=[pltpu.VMEM((hpb, dv, dk), F32)],
        compiler_params=_params("parallel", "parallel", "arbitrary"),
        name="gla",
    )(proj, proj, log_a, proj, proj, gn.reshape(GLA_HEADS, 1, dv), tab)


def _proj_ln_body(a0_ref, a1_ref, w0_ref, w1_ref, x_ref, g_ref, b_ref, o_ref, *, alpha):
    y = _dot(a0_ref[...], w0_ref[...]) + _dot(a1_ref[...], w1_ref[...])
    o_ref[...] = _layer_norm_rows(alpha * x_ref[...] + y, g_ref[...], b_ref[...])


def _proj_ln(a0, a1, w0, w1, x, g, b, alpha, tm):
    m, d = x.shape
    row = lambda i: (i, 0)
    fixed = lambda i: (0, 0)
    return pl.pallas_call(
        functools.partial(_proj_ln_body, alpha=alpha),
        grid=(m // tm,),
        in_specs=[pl.BlockSpec((tm, a0.shape[1]), row), pl.BlockSpec((tm, a1.shape[1]), row),
                  pl.BlockSpec(w0.shape, fixed), pl.BlockSpec(w1.shape, fixed),
                  pl.BlockSpec((tm, d), row), pl.BlockSpec((1, d), fixed), pl.BlockSpec((1, d), fixed)],
        out_specs=pl.BlockSpec((tm, d), row),
        out_shape=jax.ShapeDtypeStruct((m, d), F32),
        compiler_params=_params("parallel"),
        name="mixer_out_ln",
    )(a0, a1, w0, w1, x, g.reshape(1, d), b.reshape(1, d))


def _route(x, wt_ref, b_ref, idx_ref, gate_ref):
    logits = _dot_nt(wt_ref[...], x.astype(BF16)) + b_ref[...]
    n_e = logits.shape[0]
    e_iota = lax.broadcasted_iota(jnp.int32, logits.shape, 0)
    vals, idxs = [], []
    for _ in range(TOP_K):
        best = jnp.max(logits, axis=0, keepdims=True)
        sel = jnp.min(jnp.where(logits == best, e_iota, n_e), axis=0, keepdims=True)
        vals.append(best)
        idxs.append(sel)
        logits = jnp.where(e_iota == sel, -jnp.inf, logits)
    exps = [jnp.exp(v - vals[0]) for v in vals]
    total = exps[0] + exps[1] + exps[2] + exps[3]
    idx_ref[...] = jnp.concatenate(idxs, axis=0)
    gate_ref[...] = jnp.concatenate([e / total for e in exps], axis=0)


def _xattn_body(x_ref, kv_ref, wq_ref, wo_ref, g_ref, b_ref, wr_ref, br_ref, o_ref, rows_ref, idx_ref, gate_ref,
                *, alpha):
    x = x_ref[...]
    xa = wq_ref.shape[1]
    hd = xa // XA_HEADS
    q = _dot(x.astype(BF16), wq_ref[...])
    heads = []
    for h in range(XA_HEADS):
        qh = q[:, h * hd:(h + 1) * hd].astype(BF16)
        kh = kv_ref[:, h * hd:(h + 1) * hd].astype(BF16)
        vh = kv_ref[:, xa + h * hd:xa + (h + 1) * hd].astype(BF16)
        s = _dot_nt(qh, kh) * hd ** -0.5
        p = jnp.exp(s - jnp.max(s, axis=-1, keepdims=True))
        p = p / jnp.sum(p, axis=-1, keepdims=True)
        heads.append(_dot(p.astype(BF16), vh))
    o = jnp.concatenate(heads, axis=1).astype(BF16)
    y = _dot(o, wo_ref[...])
    out = _layer_norm_rows(alpha * x + y, g_ref[...], b_ref[...])
    o_ref[...] = out
    _store_row_major(rows_ref, out)
    _route(out, wr_ref, br_ref, idx_ref, gate_ref)


def _cross_attn(x, kv, wq, wo, g, b, wr_t, b_router, alpha, bsz, seq, tm):
    m, d = x.shape
    n_e = wr_t.shape[0]
    nt = seq // tm
    mem_len = kv.shape[0] // bsz
    fixed = lambda bb, i: (0, 0)
    return pl.pallas_call(
        functools.partial(_xattn_body, alpha=alpha),
        grid=(bsz, nt),
        in_specs=[pl.BlockSpec((tm, d), lambda bb, i: (bb * nt + i, 0)),
                  pl.BlockSpec((mem_len, kv.shape[1]), lambda bb, i: (bb, 0)),
                  pl.BlockSpec(wq.shape, fixed), pl.BlockSpec(wo.shape, fixed),
                  pl.BlockSpec((1, d), fixed), pl.BlockSpec((1, d), fixed),
                  pl.BlockSpec((n_e, d), fixed), pl.BlockSpec((n_e, 1), fixed)],
        out_specs=[pl.BlockSpec((tm, d), lambda bb, i: (bb * nt + i, 0)),
                   pl.BlockSpec((tm * ROW_TILES, LANE), lambda bb, i: (bb * nt + i, 0)),
                   pl.BlockSpec((TOP_K, tm), lambda bb, i: (0, bb * nt + i)),
                   pl.BlockSpec((TOP_K, tm), lambda bb, i: (0, bb * nt + i))],
        out_shape=[jax.ShapeDtypeStruct((m, d), F32),
                   jax.ShapeDtypeStruct((m * ROW_TILES, LANE), F32),
                   jax.ShapeDtypeStruct((TOP_K, m), jnp.int32),
                   jax.ShapeDtypeStruct((TOP_K, m), F32)],
        compiler_params=_params("parallel", "parallel"),
        name="cross_attn_ln",
    )(x, kv, wq, wo, g.reshape(1, d), b.reshape(1, d), wr_t, b_router.reshape(n_e, 1))


def _dispatch_tables(idx_t, n_experts):
    n_tok = idx_t.shape[1]
    n_asg = n_tok * TOP_K
    n_slots = (n_asg // MOE_SUB + 1 + n_experts) * MOE_SUB
    n_pass_max = n_asg // MOE_PASS + n_experts
    flat_e = idx_t.T.reshape(n_asg)
    onehot = (flat_e[:, None] == jnp.arange(n_experts, dtype=jnp.int32)[None, :]).astype(jnp.int32)
    csum = jnp.cumsum(onehot, axis=0)
    counts = csum[-1]
    rank = jnp.sum(csum * onehot, axis=1) - 1
    padded = (counts + MOE_SUB - 1) // MOE_SUB * MOE_SUB
    row_end = jnp.cumsum(padded)
    row_start = row_end - padded
    pos = (row_start[flat_e] + rank).astype(jnp.int32)
    asg = jnp.arange(n_asg, dtype=jnp.int32)
    slot_tok = jnp.zeros((n_slots,), jnp.int32).at[pos].set(asg // TOP_K)
    n_pass = (padded + MOE_PASS - 1) // MOE_PASS
    pass_end = jnp.cumsum(n_pass)
    pass_start = pass_end - n_pass
    total = pass_end[-1]
    i = jnp.arange(n_pass_max, dtype=jnp.int32)
    i_eff = jnp.minimum(i, total - 1)
    p_e = jnp.clip(jnp.searchsorted(pass_end, i_eff, side="right"), 0, n_experts - 1).astype(jnp.int32)
    p_off = (i_eff - pass_start[p_e]) * MOE_PASS
    p_row = (row_start[p_e] + p_off).astype(jnp.int32)
    p_chunks = jnp.clip((padded[p_e] - p_off) // MOE_SUB, 0, MOE_PASS // MOE_SUB)
    p_chunks = jnp.where(i < total, p_chunks, 0).astype(jnp.int32)
    used_rows = row_end[-1].astype(jnp.int32).reshape(1)
    return pos, slot_tok, used_rows, p_e, p_row, p_chunks, n_slots


def _gather_body(used_ref, tok_ref, tok_next_ref, x_hbm, o_ref, buf_ref, sem):
    j = pl.program_id(0)
    slot = j % 2

    def issue(toks, s):
        def pair(p, carry):
            for prio in range(2):
                r = 2 * p + prio
                src = pl.ds(pl.multiple_of(toks[0, r] * ROW_TILES, ROW_TILES), ROW_TILES)
                dst = pl.ds(pl.multiple_of(r * ROW_PITCH, 8), ROW_TILES)
                pltpu.make_async_copy(x_hbm.at[src], buf_ref.at[s, dst], sem.at[s]).start(priority=prio)
            return carry

        lax.fori_loop(0, MOE_SUB // 2, pair, 0, unroll=4)

    @pl.when(j == 0)
    def _():
        issue(tok_ref, 0)

    @pl.when((j + 1) * MOE_SUB < used_ref[0])
    def _():
        issue(tok_next_ref, 1 - slot)

    @pl.when(j * MOE_SUB < used_ref[0])
    def _():
        all_rows = pl.ds(0, MOE_SUB * ROW_TILES)
        pltpu.make_async_copy(x_hbm.at[all_rows], buf_ref.at[slot, all_rows], sem.at[slot]).wait()
        for c, part in enumerate(_load_row_major(buf_ref.at[slot], 0, MOE_SUB, ROW_PITCH)):
            o_ref[:, c * LANE:(c + 1) * LANE] = part.astype(o_ref.dtype)

    @pl.when(j * MOE_SUB >= used_ref[0])
    def _():
        o_ref[...] = jnp.zeros_like(o_ref)


def _dispatch_gather(x_rows, slot_tok, used_rows, n_slots):
    d = ROW_TILES * LANE
    n_blk = n_slots // MOE_SUB
    toks = slot_tok.reshape(n_blk, 1, MOE_SUB)
    return pl.pallas_call(
        _gather_body,
        grid_spec=pltpu.PrefetchScalarGridSpec(
            num_scalar_prefetch=1,
            grid=(n_blk,),
            in_specs=[pl.BlockSpec((None, 1, MOE_SUB), lambda j, used: (j, 0, 0),
                                   memory_space=pltpu.SMEM),
                      pl.BlockSpec((None, 1, MOE_SUB), lambda j, used: (jnp.minimum(j + 1, n_blk - 1), 0, 0),
                                   memory_space=pltpu.SMEM),
                      pl.BlockSpec(memory_space=pl.ANY)],
            out_specs=pl.BlockSpec((MOE_SUB, d), lambda j, used: (j, 0)),
            scratch_shapes=[pltpu.VMEM((2, MOE_SUB * ROW_PITCH, LANE), F32), pltpu.SemaphoreType.DMA((2,))],
        ),
        out_shape=jax.ShapeDtypeStruct((n_slots, d), BF16),
        compiler_params=_params("arbitrary"),
        name="moe_gather",
    )(used_rows, toks, toks, x_rows)


def _expert_body(e_ref, row_ref, nch_ref, used_ref, x_hbm, wgu_ref, wd_ref, bgu_ref, bd_ref, y_hbm,
                 x_buf, acc, gu_buf, wgu_bf, wd_perm, wd_bf, y_stage, x_sem, y_sem):
    i, j = pl.program_id(0), pl.program_id(1)
    n_items, last = pl.num_programs(0), pl.num_programs(1) - 1
    nch = nch_ref[i]
    slot = i % 2
    tf = wd_ref.shape[0]
    half = tf // 2

    def chunk(c):
        return pl.ds(pl.multiple_of(c * MOE_SUB, MOE_SUB), MOE_SUB)

    def x_copy(item, c):
        rows = pl.ds(pl.multiple_of(row_ref[item] + c * MOE_SUB, MOE_SUB), MOE_SUB)
        return pltpu.make_async_copy(x_hbm.at[rows], x_buf.at[item % 2, chunk(c)], x_sem.at[item % 2])

    def y_copy(first_row, s):
        rows = pl.ds(pl.multiple_of(first_row * ROW_TILES, MOE_SUB * ROW_TILES), MOE_SUB * ROW_TILES)
        return pltpu.make_async_copy(y_stage.at[s], y_hbm.at[rows], y_sem.at[s])

    def y_drain(n_chunks):
        for s in range(2):
            @pl.when(n_chunks > s)
            def _():
                y_copy(0, s).wait()

    def for_chunks(item, fn):
        def step(c, carry):
            fn(item, c)
            return carry

        lax.fori_loop(0, nch_ref[item], step, 0)

    @pl.when(j == 0)
    def _():
        @pl.when(i == 0)
        def _():
            for_chunks(0, lambda it, c: x_copy(it, c).start())

        @pl.when(i + 1 < n_items)
        def _():
            for_chunks(i + 1, lambda it, c: x_copy(it, c).start())

        for_chunks(i, lambda it, c: x_copy(it, c).wait())
        acc[...] = jnp.broadcast_to(bd_ref[...], acc.shape)

    even_lane = lax.broadcasted_iota(jnp.int32, (1, tf), 1) % 2 == 0

    def up_proj(c):
        return _dot(x_buf[slot, chunk(c), :], wgu_bf[...]) + bgu_ref[...]

    def down_proj(gu, wd):
        up = jnp.clip(pltpu.roll(gu, 2 * tf - 1, 1), -SWIGLU_LIMIT, SWIGLU_LIMIT)
        gate = jnp.minimum(gu, SWIGLU_LIMIT)
        act = (up + 1.0) * (gate / (1.0 + jnp.exp(-SWIGLU_ALPHA * gate)))
        packed = jnp.where(even_lane, act[:, :tf], pltpu.roll(act[:, tf:], 1, 1))
        return _dot(packed.astype(BF16), wd)

    wd_cur = j % 2

    def prepare_weights():
        wgu_bf[...] = wgu_ref[...].astype(BF16)
        for cb in range(wd_ref.shape[1] // LANE):
            cols = pl.ds(cb * LANE, LANE)
            wd_perm[cb, pl.ds(0, half, stride=2), :] = wd_ref[0:half, cols]
            wd_perm[cb, pl.ds(1, half, stride=2), :] = wd_ref[half:tf, cols]
            wd_bf[wd_cur, :, cols] = wd_perm[cb].astype(BF16)

    def carried_trip(k, carry):
        first = k == 0
        c_fin = jnp.where(first, nch - 1, k - 1)
        gu = gu_buf[c_fin % 2]
        gu_next = up_proj(k)
        acc[chunk(c_fin), :] += down_proj(gu, wd_bf[jnp.where(first, 1 - wd_cur, wd_cur)])
        gu_buf[k % 2] = gu_next
        return carry

    def plain_trip(k, carry, emit):
        c_fin = k - 1
        if emit:
            @pl.when(c_fin >= 2)
            def _():
                y_copy(0, c_fin % 2).wait()
        gu = gu_buf[c_fin % 2]
        gu_next = up_proj(k)
        y = down_proj(gu, wd_bf[wd_cur])
        if emit:
            _store_row_major(y_stage.at[c_fin % 2], acc[chunk(c_fin), :] + y)
            y_copy(row_ref[i] + c_fin * MOE_SUB, c_fin % 2).start()
        else:
            acc[chunk(c_fin), :] += y
        gu_buf[k % 2] = gu_next
        return carry

    @pl.when((nch > 0) & (j == 0))
    def _():
        prepare_weights()
        gu_buf[0] = up_proj(0)
        lax.fori_loop(1, nch, functools.partial(plain_trip, emit=False), 0)

    @pl.when((nch > 0) & (j > 0) & (j < last))
    def _():
        prepare_weights()
        lax.fori_loop(0, nch, carried_trip, 0)

    @pl.when(j == last)
    def _():
        @pl.when(i > 0)
        def _():
            y_drain(nch_ref[i - 1])

        @pl.when(nch > 0)
        def _():
            prepare_weights()
            carried_trip(jnp.int32(0), 0)
            lax.fori_loop(1, nch, functools.partial(plain_trip, emit=True), 0)
            c_fin = nch - 1

            @pl.when(c_fin >= 2)
            def _():
                y_copy(0, c_fin % 2).wait()

            y = down_proj(gu_buf[c_fin % 2], wd_bf[wd_cur])
            _store_row_major(y_stage.at[c_fin % 2], acc[chunk(c_fin), :] + y)
            y_copy(row_ref[i] + c_fin * MOE_SUB, c_fin % 2).start()

        @pl.when(i == n_items - 1)
        def _():
            y_drain(nch)
            y_stage[0] = jnp.zeros(y_stage.shape[1:], y_stage.dtype)

            def fill_start(c, carry):
                y_copy(c * MOE_SUB, 0).start()
                return carry

            def fill_wait(c, carry):
                y_copy(0, 0).wait()
                return carry

            first, end = used_ref[0] // MOE_SUB, y_hbm.shape[0] // (MOE_SUB * ROW_TILES)
            lax.fori_loop(first, end, fill_start, 0)
            lax.fori_loop(first, end, fill_wait, 0)


def _experts(xb, w_gu, w_down, b_gu, b_down, layer, p_e, p_row, p_chunks, used_rows, tf):
    n_slots, d = xb.shape
    n_l, n_e, f, _ = w_down.shape
    nf = f // tf
    assert nf >= 2
    n_items = p_e.shape[0]

    def jeff(i, j, nch):
        return jnp.where(nch[i] > 0, j, nf - 1)

    return pl.pallas_call(
        _expert_body,
        grid_spec=pltpu.PrefetchScalarGridSpec(
            num_scalar_prefetch=4,
            grid=(n_items, nf),
            in_specs=[
                pl.BlockSpec(memory_space=pl.ANY),
                pl.BlockSpec((None, None, d, 2 * tf), lambda i, j, e, row, nch, used: (layer, e[i], 0, jeff(i, j, nch))),
                pl.BlockSpec((None, None, tf, d), lambda i, j, e, row, nch, used: (layer, e[i], jeff(i, j, nch), 0)),
                pl.BlockSpec((None, None, 1, 2 * tf), lambda i, j, e, row, nch, used: (layer, e[i], 0, jeff(i, j, nch))),
                pl.BlockSpec((None, None, 1, d), lambda i, j, e, row, nch, used: (layer, e[i], 0, 0)),
            ],
            out_specs=pl.BlockSpec(memory_space=pl.ANY),
            scratch_shapes=[pltpu.VMEM((2, MOE_PASS, d), BF16), pltpu.VMEM((MOE_PASS, d), F32),
                            pltpu.VMEM((2, MOE_SUB, 2 * tf), F32), pltpu.VMEM((d, 2 * tf), BF16),
                            pltpu.VMEM((d // LANE, tf, LANE), F32), pltpu.VMEM((2, tf, d), BF16),
                            pltpu.VMEM((2, MOE_SUB * ROW_TILES, LANE), F32),
                            pltpu.SemaphoreType.DMA((2,)), pltpu.SemaphoreType.DMA((2,))],
        ),
        out_shape=jax.ShapeDtypeStruct((n_slots * ROW_TILES, LANE), F32),
        compiler_params=_params("arbitrary", "arbitrary"),
        name="moe_experts",
    )(p_e, p_row, p_chunks, used_rows, xb, w_gu, w_down, b_gu.reshape(n_l, n_e, 1, 2 * f),
      b_down.reshape(n_l, n_e, 1, d))


def _combine_body(pos_ref, pos_next_ref, x_ref, gate_ref, y_hbm, g_ref, b_ref, o_ref, buf_ref, y_ref, sem,
                  *, alpha):
    tm = x_ref.shape[0]
    i = pl.program_id(0)
    slot = i % 2

    def issue(pos, s):
        for k in range(TOP_K):
            def pair(p, carry, k=k):
                for prio in range(2):
                    t = 2 * p + prio
                    src = pl.ds(pl.multiple_of(pos[k, t] * ROW_TILES, ROW_TILES), ROW_TILES)
                    dst = pl.ds(pl.multiple_of((k * tm + t) * ROW_PITCH, 8), ROW_TILES)
                    pltpu.make_async_copy(y_hbm.at[src], buf_ref.at[s, dst], sem.at[s]).start(priority=prio)
                return carry

            lax.fori_loop(0, tm // 2, pair, 0, unroll=4)

    @pl.when(i == 0)
    def _():
        issue(pos_ref, 0)

    @pl.when(i + 1 < pl.num_programs(0))
    def _():
        issue(pos_next_ref, 1 - slot)

    all_rows = pl.ds(0, TOP_K * tm * ROW_TILES)
    pltpu.make_async_copy(y_hbm.at[all_rows], buf_ref.at[slot, all_rows], sem.at[slot]).wait()
    gate = gate_ref[...]
    parts = [_load_row_major(buf_ref.at[slot], k * tm * ROW_PITCH, tm, ROW_PITCH) for k in range(TOP_K)]
    for c in range(ROW_TILES):
        y_ref[:, c * LANE:(c + 1) * LANE] = (
            (gate[:, 0:1] * parts[0][c] + gate[:, 1:2] * parts[1][c])
            + (gate[:, 2:3] * parts[2][c] + gate[:, 3:4] * parts[3][c]))
    o_ref[...] = _layer_norm_rows(alpha * x_ref[...] + y_ref[...], g_ref[...], b_ref[...])


def _combine_ln(x, yb_rows, pos, gate, g, b, alpha, tm):
    m, d = x.shape
    nt = m // tm
    pos_t = pos.reshape(nt, tm, TOP_K).transpose(0, 2, 1)
    return pl.pallas_call(
        functools.partial(_combine_body, alpha=alpha),
        grid=(nt,),
        in_specs=[pl.BlockSpec((None, TOP_K, tm), lambda i: (i, 0, 0), memory_space=pltpu.SMEM),
                  pl.BlockSpec((None, TOP_K, tm), lambda i: (jnp.minimum(i + 1, nt - 1), 0, 0),
                               memory_space=pltpu.SMEM),
                  pl.BlockSpec((tm, d), lambda i: (i, 0)),
                  pl.BlockSpec((tm, TOP_K), lambda i: (i, 0)),
                  pl.BlockSpec(memory_space=pl.ANY),
                  pl.BlockSpec((1, d), lambda i: (0, 0)), pl.BlockSpec((1, d), lambda i: (0, 0))],
        out_specs=pl.BlockSpec((tm, d), lambda i: (i, 0)),
        out_shape=jax.ShapeDtypeStruct((m, d), F32),
        scratch_shapes=[pltpu.VMEM((2, TOP_K * tm * ROW_PITCH, LANE), F32), pltpu.VMEM((tm, d), F32),
                        pltpu.SemaphoreType.DMA((2,))],
        compiler_params=_params("arbitrary"),
        name="moe_combine_ln",
    )(pos_t, pos_t, x, gate, yb_rows, g.reshape(1, d), b.reshape(1, d))


def _mixer(x, w_in, conv_w, w_gate2, b_gate2, gla_norm_g, w_out, ln_g, ln_b, alpha, bsz, seq):
    d = x.shape[1]
    ch = conv_w.shape[1]
    dk_total = w_gate2.shape[1]
    dk = dk_total // GLA_HEADS
    dv = gla_norm_g.shape[1]
    main_cols = w_in.shape[1] - GLA_GATE_RANK
    w1p = jnp.zeros((d, LANE), BF16).at[:, :GLA_GATE_RANK].set(w_in[:, main_cols:].astype(BF16))
    w2p = jnp.zeros((LANE, dk_total), BF16).at[:GLA_GATE_RANK, :].set(w_gate2.astype(BF16))
    proj, log_a = _in_proj(x, w_in[:, :main_cols].astype(BF16), w1p, w2p, b_gate2.reshape(1, dk_total),
                           tm=1024, tn=512)
    y_conv = _short_conv(proj, conv_w, bsz, seq, ch, tt=512, cw=512)
    q_off = 3 * ch
    k_off = q_off + dk_total
    v_off = k_off + dk_total
    r_off = v_off + GLA_HEADS * dv
    y_gla = _gla(proj, log_a, gla_norm_g, bsz, seq, q_off, k_off, v_off, r_off, dk, dv, tt=512, hpb=2)
    w_out_bf = w_out.astype(BF16)
    return _proj_ln(y_conv, y_gla, w_out_bf[:ch], w_out_bf[ch:], x, ln_g, ln_b, alpha, tm=512)


def _moe(x, x_rows, idx_t, gate_t, layer, w_gu, b_gu, w_down, b_down, ln_g, ln_b, alpha):
    n_e = w_gu.shape[1]
    pos, slot_tok, used_rows, p_e, p_row, p_chunks, n_slots = _dispatch_tables(idx_t, n_e)
    xb = _dispatch_gather(x_rows, slot_tok, used_rows, n_slots)
    yb = _experts(xb, w_gu, w_down, b_gu, b_down, layer, p_e, p_row, p_chunks, used_rows, tf=256)
    return _combine_ln(x, yb, pos, gate_t.T, ln_g, ln_b, alpha, tm=128)


def kernel(x, mem, w_in, conv_w, w_gate2, b_gate2, gla_norm_g, w_out, ln_mix_g, ln_mix_b, w_xq, w_xkv, w_xo, ln_xa_g, ln_xa_b, w_router, b_router, w_gu, b_gu, w_down, b_down, ln_moe_g, ln_moe_b):
    bsz, seq, d = x.shape
    depth = w_in.shape[0]
    alpha = (2 * depth) ** 0.25
    xf = x.reshape(bsz * seq, d)
    memf = mem.reshape(bsz * mem.shape[1], d)
    for l in range(depth):
        xf = _mixer(xf, w_in[l], conv_w[l], w_gate2[l], b_gate2[l], gla_norm_g[l], w_out[l],
                    ln_mix_g[l], ln_mix_b[l], alpha, bsz, seq)
        kv = _matmul(memf, w_xkv[l].astype(BF16), tm=memf.shape[0], tn=512, name="xattn_kv_proj")
        xf, x_rows, idx_t, gate_t = _cross_attn(
            xf, kv, w_xq[l].astype(BF16), w_xo[l].astype(BF16), ln_xa_g[l], ln_xa_b[l],
            w_router[l].T.astype(BF16), b_router[l], alpha, bsz, seq, tm=512)
        xf = _moe(xf, x_rows, idx_t, gate_t, l, w_gu, b_gu, w_down, b_down, ln_moe_g[l], ln_moe_b[l], alpha)
    return xf.reshape(bsz, seq, d)
```

```python
import functools

import numpy as np
import jax
import jax.numpy as jnp
from jax import lax
from jax.experimental import pallas as pl
from jax.experimental.pallas import tpu as pltpu

CONV_W = 3
GLA_HEADS = 4
GLA_GATE_RANK = 16
GLA_TAU = 16.0
XA_HEADS = 4
TOP_K = 4
SWIGLU_LIMIT = 7.0
SWIGLU_ALPHA = 1.702
LN_EPS = 1e-5
NORM_EPS = 1e-6

LANE = 128
GLA_CHUNK = 128
GLA_LEVELS = 7
MOE_SUB = 272
MOE_PASS = 6 * MOE_SUB
ROW_TILES = 16
ROW_PITCH = 24
VMEM_LIMIT = 56 * 1024 * 1024

BF16 = jnp.bfloat16
F32 = jnp.float32


def _dot(a, b):
    return jnp.dot(a, b, preferred_element_type=F32)


def _dot_nt(a, b):
    return lax.dot_general(a, b, (((1,), (1,)), ((), ())), preferred_element_type=F32)


def _dot_tn(a, b):
    return lax.dot_general(a, b, (((0,), (0,)), ((), ())), preferred_element_type=F32)


def _params(*sem):
    return pltpu.CompilerParams(dimension_semantics=sem, vmem_limit_bytes=VMEM_LIMIT)


def _store_row_major(dst_ref, val):
    n = val.shape[0]
    for c in range(ROW_TILES):
        dst_ref[pl.ds(c, n, stride=ROW_TILES), :] = val[:, c * LANE:(c + 1) * LANE]


def _load_row_major(src_ref, first, n, pitch):
    return [src_ref[pl.ds(first + c, n, stride=pitch), :] for c in range(ROW_TILES)]


def _layer_norm_rows(y, g, b):
    mu = jnp.mean(y, axis=-1, keepdims=True)
    yc = y - mu
    var = jnp.mean(yc * yc, axis=-1, keepdims=True)
    return yc * lax.rsqrt(var + LN_EPS) * g + b


def _matmul_body(a_ref, w_ref, o_ref, abf_ref):
    @pl.when(pl.program_id(1) == 0)
    def _():
        abf_ref[...] = a_ref[...].astype(BF16)

    o_ref[...] = _dot(abf_ref[...], w_ref[...]).astype(o_ref.dtype)


def _matmul(a, w, tm, tn, name):
    m, k = a.shape
    n = w.shape[1]
    return pl.pallas_call(
        _matmul_body,
        grid=(m // tm, n // tn),
        in_specs=[pl.BlockSpec((tm, k), lambda i, j: (i, 0)),
                  pl.BlockSpec((k, tn), lambda i, j: (0, j))],
        out_specs=pl.BlockSpec((tm, tn), lambda i, j: (i, j)),
        out_shape=jax.ShapeDtypeStruct((m, n), F32),
        scratch_shapes=[pltpu.VMEM((tm, k), BF16)],
        compiler_params=_params("parallel", "arbitrary"),
        name=name,
    )(a, w)


def _in_proj_body(a_ref, w_ref, w1_ref, w2_ref, b_ref, o_ref, gate_ref, abf_ref):
    @pl.when(pl.program_id(1) == 0)
    def _():
        abf_ref[...] = a_ref[...].astype(BF16)
        low = _dot(abf_ref[...], w1_ref[...])
        z = _dot(low.astype(BF16), w2_ref[...]) + b_ref[...]
        log_sig = jnp.minimum(z, 0.0) - jnp.log1p(jnp.exp(-jnp.abs(z)))
        gate_ref[...] = log_sig / GLA_TAU

    o_ref[...] = _dot(abf_ref[...], w_ref[...])


def _in_proj(a, w, w1p, w2p, b2, tm, tn):
    m, k = a.shape
    n = w.shape[1]
    ng = w2p.shape[1]
    fixed = lambda i, j: (0, 0)
    return pl.pallas_call(
        _in_proj_body,
        grid=(m // tm, n // tn),
        in_specs=[pl.BlockSpec((tm, k), lambda i, j: (i, 0)),
                  pl.BlockSpec((k, tn), lambda i, j: (0, j)),
                  pl.BlockSpec(w1p.shape, fixed), pl.BlockSpec(w2p.shape, fixed),
                  pl.BlockSpec((1, ng), fixed)],
        out_specs=[pl.BlockSpec((tm, tn), lambda i, j: (i, j)),
                   pl.BlockSpec((tm, ng), lambda i, j: (i, 0))],
        out_shape=[jax.ShapeDtypeStruct((m, n), F32), jax.ShapeDtypeStruct((m, ng), F32)],
        scratch_shapes=[pltpu.VMEM((tm, k), BF16)],
        compiler_params=_params("parallel", "arbitrary"),
        name="mixer_in_proj",
    )(a, w, w1p, w2p, b2)


def _conv_body(gc_ref, gb_ref, h_ref, gch_ref, hh_ref, w_ref, o_ref):
    u = gc_ref[...] * h_ref[...]
    uh = jnp.where(pl.program_id(1) == 0, 0.0, gch_ref[...] * hh_ref[...])
    row = lax.broadcasted_iota(jnp.int32, u.shape, 0)
    u1 = jnp.where(row == 0, uh[7:8, :], pltpu.roll(u, 1, 0))
    u2 = pltpu.roll(u, 2, 0)
    u2 = jnp.where(row == 0, uh[6:7, :], jnp.where(row == 1, uh[7:8, :], u2))
    w = w_ref[...]
    conv = w[0:1, :] * u2 + w[1:2, :] * u1 + w[2:3, :] * u
    o_ref[...] = (gb_ref[...] * conv).astype(o_ref.dtype)


def _short_conv(proj, conv_w, bsz, seq, ch, tt, cw):
    nt, nc = seq // tt, ch // cw

    def main(off):
        return pl.BlockSpec((tt, cw), lambda b, i, c: (b * nt + i, off * nc + c))

    def halo(off):
        return pl.BlockSpec(
            (8, cw), lambda b, i, c: (jnp.maximum((b * seq + i * tt) // 8 - 1, 0), off * nc + c))

    return pl.pallas_call(
        _conv_body,
        grid=(bsz, nt, nc),
        in_specs=[main(0), main(1), main(2), halo(0), halo(2),
                  pl.BlockSpec((CONV_W, cw), lambda b, i, c: (0, c))],
        out_specs=pl.BlockSpec((tt, cw), lambda b, i, c: (b * nt + i, c)),
        out_shape=jax.ShapeDtypeStruct((bsz * seq, ch), BF16),
        compiler_params=_params("parallel", "parallel", "parallel"),
        name="short_conv",
    )(proj, proj, proj, proj, proj, conv_w)


def _gla_tables():
    c = GLA_CHUNK
    i = np.arange(c)[:, None]
    t = np.arange(c)[None, :]
    mats = []
    for lvl in range(GLA_LEVELS):
        m = ((i >> (lvl + 1)) << (lvl + 1)) + (1 << lvl) - 1
        upper = (i > m) & (t > m) & (t <= i)
        lower = (i <= m) & (t > i) & (t <= m)
        mats.append(upper | lower)
    mats.append(t <= i)
    return jnp.asarray(np.concatenate(mats, axis=0).astype(np.float32), dtype=BF16)


def _gla_body(q_ref, k_ref, g_ref, v_ref, r_ref, gn_ref, tab_ref, o_ref, st_ref, *, dk_scale):
    c = GLA_CHUNK
    n_heads = st_ref.shape[0]
    dv, dk = st_ref.shape[1:]

    @pl.when(pl.program_id(2) == 0)
    def _():
        st_ref[...] = jnp.zeros_like(st_ref)

    ii = lax.broadcasted_iota(jnp.int32, (c, c), 0)
    jj = lax.broadcasted_iota(jnp.int32, (c, c), 1)
    xor = ii ^ jj
    lower = ii > jj

    for ch in range(q_ref.shape[0] // c):
        rows = pl.ds(ch * c, c)
        for hh in range(n_heads):
            kcols = pl.ds(hh * dk, dk)
            vcols = pl.ds(hh * dv, dv)
            q = q_ref[rows, kcols] * dk_scale
            k = k_ref[rows, kcols]
            g = g_ref[rows, kcols]
            v_bf = v_ref[rows, vcols].astype(BF16)
            g_hi = g.astype(BF16)
            g_lo = (g - g_hi.astype(F32)).astype(BF16)
            sums = _dot(tab_ref[...], jnp.concatenate([g_hi, g_lo], axis=1))
            sums = sums[:, :dk] + sums[:, dk:]

            scores = jnp.where(ii == jj, _dot_nt(q.astype(BF16), k.astype(BF16)), 0.0)
            for lvl in range(GLA_LEVELS):
                w = jnp.exp(sums[lvl * c:(lvl + 1) * c, :])
                s_l = _dot_nt((q * w).astype(BF16), (k * w).astype(BF16))
                scores = jnp.where(((xor >> lvl) == 1) & lower, s_l, scores)

            b = sums[GLA_LEVELS * c:, :]
            b_last = b[c - 1:c, :]
            st = st_ref[hh]
            o = _dot_nt((q * jnp.exp(b)).astype(BF16), st.astype(BF16)) + _dot(scores.astype(BF16), v_bf)
            k_dec = (k * jnp.exp(b_last - b)).astype(BF16)
            st_ref[hh] = st * jnp.exp(b_last) + _dot_tn(v_bf, k_dec)

            o = o * lax.rsqrt(jnp.mean(o * o, axis=-1, keepdims=True) + NORM_EPS) * gn_ref[hh]
            r = r_ref[rows, vcols]
            o_ref[rows, vcols] = (o * (r / (1.0 + jnp.exp(-r)))).astype(o_ref.dtype)


def _gla(proj, log_a, gn, bsz, seq, q_off, k_off, v_off, r_off, dk, dv, tt, hpb):
    nt = seq // tt
    tab = _gla_tables()

    def spec(width, off):
        return pl.BlockSpec((tt, hpb * width), lambda b, h, i: (b * nt + i, off // (hpb * width) + h))

    return pl.pallas_call(
        functools.partial(_gla_body, dk_scale=dk ** -0.5),
        grid=(bsz, GLA_HEADS // hpb, nt),
        in_specs=[spec(dk, q_off), spec(dk, k_off), spec(dk, 0), spec(dv, v_off), spec(dv, r_off),
                  pl.BlockSpec((hpb, 1, dv), lambda b, h, i: (h, 0, 0)),
                  pl.BlockSpec(tab.shape, lambda b, h, i: (0, 0))],
        out_specs=pl.BlockSpec((tt, hpb * dv), lambda b, h, i: (b * nt + i, h)),
        out_shape=jax.ShapeDtypeStruct((bsz * seq, GLA_HEADS * dv), BF16),
        scratch_shapes=[pltpu.VMEM((hpb, dv, dk), F32)],
        compiler_params=_params("parallel", "parallel", "arbitrary"),
        name="gla",
    )(proj, proj, log_a, proj, proj, gn.reshape(GLA_HEADS, 1, dv), tab)


def _proj_ln_body(a0_ref, a1_ref, w0_ref, w1_ref, x_ref, g_ref, b_ref, o_ref, *, alpha):
    y = _dot(a0_ref[...], w0_ref[...]) + _dot(a1_ref[...], w1_ref[...])
    o_ref[...] = _layer_norm_rows(alpha * x_ref[...] + y, g_ref[...], b_ref[...])


def _proj_ln(a0, a1, w0, w1, x, g, b, alpha, tm):
    m, d = x.shape
    row = lambda i: (i, 0)
    fixed = lambda i: (0, 0)
    return pl.pallas_call(
        functools.partial(_proj_ln_body, alpha=alpha),
        grid=(m // tm,),
        in_specs=[pl.BlockSpec((tm, a0.shape[1]), row), pl.BlockSpec((tm, a1.shape[1]), row),
                  pl.BlockSpec(w0.shape, fixed), pl.BlockSpec(w1.shape, fixed),
                  pl.BlockSpec((tm, d), row), pl.BlockSpec((1, d), fixed), pl.BlockSpec((1, d), fixed)],
        out_specs=pl.BlockSpec((tm, d), row),
        out_shape=jax.ShapeDtypeStruct((m, d), F32),
        compiler_params=_params("parallel"),
        name="mixer_out_ln",
    )(a0, a1, w0, w1, x, g.reshape(1, d), b.reshape(1, d))


def _route(x, wt_ref, b_ref, idx_ref, gate_ref):
    logits = _dot_nt(wt_ref[...], x.astype(BF16)) + b_ref[...]
    n_e = logits.shape[0]
    e_iota = lax.broadcasted_iota(jnp.int32, logits.shape, 0)
    vals, idxs = [], []
    for _ in range(TOP_K):
        best = jnp.max(logits, axis=0, keepdims=True)
        sel = jnp.min(jnp.where(logits == best, e_iota, n_e), axis=0, keepdims=True)
        vals.append(best)
        idxs.append(sel)
        logits = jnp.where(e_iota == sel, -jnp.inf, logits)
    exps = [jnp.exp(v - vals[0]) for v in vals]
    total = exps[0] + exps[1] + exps[2] + exps[3]
    idx_ref[...] = jnp.concatenate(idxs, axis=0)
    gate_ref[...] = jnp.concatenate([e / total for e in exps], axis=0)


def _xattn_body(x_ref, kv_ref, wq_ref, wo_ref, g_ref, b_ref, wr_ref, br_ref, o_ref, rows_ref, idx_ref, gate_ref,
                *, alpha):
    x = x_ref[...]
    xa = wq_ref.shape[1]
    hd = xa // XA_HEADS
    q = _dot(x.astype(BF16), wq_ref[...])
    heads = []
    for h in range(XA_HEADS):
        qh = q[:, h * hd:(h + 1) * hd].astype(BF16)
        kh = kv_ref[:, h * hd:(h + 1) * hd].astype(BF16)
        vh = kv_ref[:, xa + h * hd:xa + (h + 1) * hd].astype(BF16)
        s = _dot_nt(qh, kh) * hd ** -0.5
        p = jnp.exp(s - jnp.max(s, axis=-1, keepdims=True))
        p = p / jnp.sum(p, axis=-1, keepdims=True)
        heads.append(_dot(p.astype(BF16), vh))
    o = jnp.concatenate(heads, axis=1).astype(BF16)
    y = _dot(o, wo_ref[...])
    out = _layer_norm_rows(alpha * x + y, g_ref[...], b_ref[...])
    o_ref[...] = out
    _store_row_major(rows_ref, out)
    _route(out, wr_ref, br_ref, idx_ref, gate_ref)


def _cross_attn(x, kv, wq, wo, g, b, wr_t, b_router, alpha, bsz, seq, tm):
    m, d = x.shape
    n_e = wr_t.shape[0]
    nt = seq // tm
    mem_len = kv.shape[0] // bsz
    fixed = lambda bb, i: (0, 0)
    return pl.pallas_call(
        functools.partial(_xattn_body, alpha=alpha),
        grid=(bsz, nt),
        in_specs=[pl.BlockSpec((tm, d), lambda bb, i: (bb * nt + i, 0)),
                  pl.BlockSpec((mem_len, kv.shape[1]), lambda bb, i: (bb, 0)),
                  pl.BlockSpec(wq.shape, fixed), pl.BlockSpec(wo.shape, fixed),
                  pl.BlockSpec((1, d), fixed), pl.BlockSpec((1, d), fixed),
                  pl.BlockSpec((n_e, d), fixed), pl.BlockSpec((n_e, 1), fixed)],
        out_specs=[pl.BlockSpec((tm, d), lambda bb, i: (bb * nt + i, 0)),
                   pl.BlockSpec((tm * ROW_TILES, LANE), lambda bb, i: (bb * nt + i, 0)),
                   pl.BlockSpec((TOP_K, tm), lambda bb, i: (0, bb * nt + i)),
                   pl.BlockSpec((TOP_K, tm), lambda bb, i: (0, bb * nt + i))],
        out_shape=[jax.ShapeDtypeStruct((m, d), F32),
                   jax.ShapeDtypeStruct((m * ROW_TILES, LANE), F32),
                   jax.ShapeDtypeStruct((TOP_K, m), jnp.int32),
                   jax.ShapeDtypeStruct((TOP_K, m), F32)],
        compiler_params=_params("parallel", "parallel"),
        name="cross_attn_ln",
    )(x, kv, wq, wo, g.reshape(1, d), b.reshape(1, d), wr_t, b_router.reshape(n_e, 1))


def _dispatch_tables(idx_t, n_experts):
    n_tok = idx_t.shape[1]
    n_asg = n_tok * TOP_K
    n_slots = (n_asg // MOE_SUB + 1 + n_experts) * MOE_SUB
    n_pass_max = n_asg // MOE_PASS + n_experts
    flat_e = idx_t.T.reshape(n_asg)
    onehot = (flat_e[:, None] == jnp.arange(n_experts, dtype=jnp.int32)[None, :]).astype(jnp.int32)
    csum = jnp.cumsum(onehot, axis=0)
    counts = csum[-1]
    rank = jnp.sum(csum * onehot, axis=1) - 1
    padded = (counts + MOE_SUB - 1) // MOE_SUB * MOE_SUB
    row_end = jnp.cumsum(padded)
    row_start = row_end - padded
    pos = (row_start[flat_e] + rank).astype(jnp.int32)
    asg = jnp.arange(n_asg, dtype=jnp.int32)
    slot_tok = jnp.zeros((n_slots,), jnp.int32).at[pos].set(asg // TOP_K)
    n_pass = (padded + MOE_PASS - 1) // MOE_PASS
    pass_end = jnp.cumsum(n_pass)
    pass_start = pass_end - n_pass
    total = pass_end[-1]
    i = jnp.arange(n_pass_max, dtype=jnp.int32)
    i_eff = jnp.minimum(i, total - 1)
    p_e = jnp.clip(jnp.searchsorted(pass_end, i_eff, side="right"), 0, n_experts - 1).astype(jnp.int32)
    p_off = (i_eff - pass_start[p_e]) * MOE_PASS
    p_row = (row_start[p_e] + p_off).astype(jnp.int32)
    p_chunks = jnp.clip((padded[p_e] - p_off) // MOE_SUB, 0, MOE_PASS // MOE_SUB)
    p_chunks = jnp.where(i < total, p_chunks, 0).astype(jnp.int32)
    used_rows = row_end[-1].astype(jnp.int32).reshape(1)
    return pos, slot_tok, used_rows, p_e, p_row, p_chunks, n_slots


def _gather_body(used_ref, tok_ref, tok_next_ref, x_hbm, o_ref, buf_ref, sem):
    j = pl.program_id(0)
    slot = j % 2

    def issue(toks, s):
        def pair(p, carry):
            for prio in range(2):
                r = 2 * p + prio
                src = pl.ds(pl.multiple_of(toks[0, r] * ROW_TILES, ROW_TILES), ROW_TILES)
                dst = pl.ds(pl.multiple_of(r * ROW_PITCH, 8), ROW_TILES)
                pltpu.make_async_copy(x_hbm.at[src], buf_ref.at[s, dst], sem.at[s]).start(priority=prio)
            return carry

        lax.fori_loop(0, MOE_SUB // 2, pair, 0, unroll=4)

    @pl.when(j == 0)
    def _():
        issue(tok_ref, 0)

    @pl.when((j + 1) * MOE_SUB < used_ref[0])
    def _():
        issue(tok_next_ref, 1 - slot)

    @pl.when(j * MOE_SUB < used_ref[0])
    def _():
        all_rows = pl.ds(0, MOE_SUB * ROW_TILES)
        pltpu.make_async_copy(x_hbm.at[all_rows], buf_ref.at[slot, all_rows], sem.at[slot]).wait()
        for c, part in enumerate(_load_row_major(buf_ref.at[slot], 0, MOE_SUB, ROW_PITCH)):
            o_ref[:, c * LANE:(c + 1) * LANE] = part.astype(o_ref.dtype)

    @pl.when(j * MOE_SUB >= used_ref[0])
    def _():
        o_ref[...] = jnp.zeros_like(o_ref)


def _dispatch_gather(x_rows, slot_tok, used_rows, n_slots):
    d = ROW_TILES * LANE
    n_blk = n_slots // MOE_SUB
    toks = slot_tok.reshape(n_blk, 1, MOE_SUB)
    return pl.pallas_call(
        _gather_body,
        grid_spec=pltpu.PrefetchScalarGridSpec(
            num_scalar_prefetch=1,
            grid=(n_blk,),
            in_specs=[pl.BlockSpec((None, 1, MOE_SUB), lambda j, used: (j, 0, 0),
                                   memory_space=pltpu.SMEM),
                      pl.BlockSpec((None, 1, MOE_SUB), lambda j, used: (jnp.minimum(j + 1, n_blk - 1), 0, 0),
                                   memory_space=pltpu.SMEM),
                      pl.BlockSpec(memory_space=pl.ANY)],
            out_specs=pl.BlockSpec((MOE_SUB, d), lambda j, used: (j, 0)),
            scratch_shapes=[pltpu.VMEM((2, MOE_SUB * ROW_PITCH, LANE), F32), pltpu.SemaphoreType.DMA((2,))],
        ),
        out_shape=jax.ShapeDtypeStruct((n_slots, d), BF16),
        compiler_params=_params("arbitrary"),
        name="moe_gather",
    )(used_rows, toks, toks, x_rows)


PREP_PARTS = 4


def _expert_body(e_ref, row_ref, nch_ref, used_ref, x_hbm, wgu_ref, wd_ref, bgu_ref, bd_ref, y_hbm,
                 x_buf, acc, gu_buf, wgu_bf, wd_perm, wd_bf, bgu_buf, y_stage, x_sem, y_sem, *, nf):
    s = pl.program_id(0)
    n_tiles = pl.num_programs(0) - 1
    n_items = n_tiles // nf
    last = nf - 1
    d = wd_ref.shape[1]
    tf = wd_ref.shape[0]
    half = tf // 2

    t = jnp.maximum(s - 1, 0)
    i, j = t // nf, t % nf
    nch = jnp.where(s >= 1, nch_ref[i], 0)
    slot = i % 2
    w_cur, wd_cur, wd_prev = t % 2, t % 3, (t + 2) % 3
    w_nxt, wd_nxt = s % 2, s % 3

    def chunk(c):
        return pl.ds(pl.multiple_of(c * MOE_SUB, MOE_SUB), MOE_SUB)

    def x_copy(item, c):
        rows = pl.ds(pl.multiple_of(row_ref[item] + c * MOE_SUB, MOE_SUB), MOE_SUB)
        return pltpu.make_async_copy(x_hbm.at[rows], x_buf.at[item % 2, chunk(c)], x_sem.at[item % 2])

    def y_copy(first_row, st):
        rows = pl.ds(pl.multiple_of(first_row * ROW_TILES, MOE_SUB * ROW_TILES), MOE_SUB * ROW_TILES)
        return pltpu.make_async_copy(y_stage.at[st], y_hbm.at[rows], y_sem.at[st])

    def y_drain(n_chunks):
        for st in range(2):
            @pl.when(n_chunks > st)
            def _():
                y_copy(0, st).wait()

    def for_chunks(item, fn):
        def step(c, carry):
            fn(item, c)
            return carry

        lax.fori_loop(0, nch_ref[item], step, 0)

    @pl.when((s >= 1) & (j == 0))
    def _():
        @pl.when(i == 0)
        def _():
            for_chunks(0, lambda it, c: x_copy(it, c).start())

        @pl.when(i + 1 < n_items)
        def _():
            for_chunks(i + 1, lambda it, c: x_copy(it, c).start())

        for_chunks(i, lambda it, c: x_copy(it, c).wait())
        acc[...] = jnp.broadcast_to(bd_ref[...], acc.shape)

    def prepare_part(p):
        p = jnp.minimum(p, PREP_PARTS - 1)
        gr = d // PREP_PARTS
        rows = pl.ds(pl.multiple_of(p * gr, gr), gr)
        wgu_bf[w_nxt, rows, :] = wgu_ref[rows, :].astype(BF16)
        q = half // PREP_PARTS
        src_a = pl.ds(pl.multiple_of(p * q, q), q)
        src_b = pl.ds(pl.multiple_of(half + p * q, q), q)
        dst = pl.ds(pl.multiple_of(2 * p * q, 2 * q), 2 * q)
        for cb in range(d // LANE):
            cols = pl.ds(cb * LANE, LANE)
            wd_perm[cb, pl.ds(2 * p * q, q, stride=2), :] = wd_ref[src_a, cols]
            wd_perm[cb, pl.ds(2 * p * q + 1, q, stride=2), :] = wd_ref[src_b, cols]
            wd_bf[wd_nxt, dst, cols] = wd_perm[cb, dst, :].astype(BF16)
        bgu_buf[w_nxt] = bgu_ref[...]

    even_lane = lax.broadcasted_iota(jnp.int32, (1, tf), 1) % 2 == 0

    def up_proj(c):
        return _dot(x_buf[slot, chunk(c), :], wgu_bf[w_cur]) + bgu_buf[w_cur]

    def down_proj(gu, wd):
        up = jnp.clip(pltpu.roll(gu, 2 * tf - 1, 1), -SWIGLU_LIMIT, SWIGLU_LIMIT)
        gate = jnp.minimum(gu, SWIGLU_LIMIT)
        act = (up + 1.0) * (gate / (1.0 + jnp.exp(-SWIGLU_ALPHA * gate)))
        packed = jnp.where(even_lane, act[:, :tf], pltpu.roll(act[:, tf:], 1, 1))
        return _dot(packed.astype(BF16), wd)

    def carried_trip(k, carry):
        first = k == 0
        c_fin = jnp.where(first, nch - 1, k - 1)
        gu = gu_buf[c_fin % 2]
        gu_next = up_proj(k)
        acc[chunk(c_fin), :] += down_proj(gu, wd_bf[jnp.where(first, wd_prev, wd_cur)])
        gu_buf[k % 2] = gu_next
        prepare_part(k)
        return carry

    def plain_trip(k, carry, emit, part_of):
        c_fin = k - 1
        if emit:
            @pl.when(c_fin >= 2)
            def _():
                y_copy(0, c_fin % 2).wait()
        gu = gu_buf[c_fin % 2]
        gu_next = up_proj(k)
        y = down_proj(gu, wd_bf[wd_cur])
        if emit:
            _store_row_major(y_stage.at[c_fin % 2], acc[chunk(c_fin), :] + y)
            y_copy(row_ref[i] + c_fin * MOE_SUB, c_fin % 2).start()
        else:
            acc[chunk(c_fin), :] += y
        gu_buf[k % 2] = gu_next
        prepare_part(part_of(k))
        return carry

    def prepare_rest(n_done):
        for p in range(PREP_PARTS):
            @pl.when(n_done <= p)
            def _():
                prepare_part(jnp.int32(p))

    @pl.when(nch == 0)
    def _():
        prepare_rest(0)

    @pl.when((nch > 0) & (j == 0))
    def _():
        gu_buf[0] = up_proj(0)
        lax.fori_loop(1, nch, functools.partial(plain_trip, emit=False, part_of=lambda k: k - 1), 0)
        prepare_rest(nch - 1)

    @pl.when((nch > 0) & (j > 0) & (j < last))
    def _():
        lax.fori_loop(0, nch, carried_trip, 0)
        prepare_rest(nch)

    @pl.when((s >= 1) & (j == last))
    def _():
        @pl.when(i > 0)
        def _():
            y_drain(nch_ref[i - 1])

        @pl.when(nch > 0)
        def _():
            carried_trip(jnp.int32(0), 0)
            lax.fori_loop(1, nch, functools.partial(plain_trip, emit=True, part_of=lambda k: k), 0)
            c_fin = nch - 1

            @pl.when(c_fin >= 2)
            def _():
                y_copy(0, c_fin % 2).wait()

            y = down_proj(gu_buf[c_fin % 2], wd_bf[wd_cur])
            _store_row_major(y_stage.at[c_fin % 2], acc[chunk(c_fin), :] + y)
            y_copy(row_ref[i] + c_fin * MOE_SUB, c_fin % 2).start()
            prepare_rest(nch)

        @pl.when(i == n_items - 1)
        def _():
            y_drain(nch)
            y_stage[0] = jnp.zeros(y_stage.shape[1:], y_stage.dtype)

            def fill_start(c, carry):
                y_copy(c * MOE_SUB, 0).start()
                return carry

            def fill_wait(c, carry):
                y_copy(0, 0).wait()
                return carry

            first, end = used_ref[0] // MOE_SUB, y_hbm.shape[0] // (MOE_SUB * ROW_TILES)
            lax.fori_loop(first, end, fill_start, 0)
            lax.fori_loop(first, end, fill_wait, 0)


def _experts(xb, w_gu, w_down, b_gu, b_down, layer, p_e, p_row, p_chunks, used_rows, tf):
    n_slots, d = xb.shape
    n_l, n_e, f, _ = w_down.shape
    nf = f // tf
    assert nf >= 2
    assert d % (PREP_PARTS * 16) == 0 and tf % (2 * PREP_PARTS * 16) == 0
    n_tiles = p_e.shape[0] * nf

    def tile(s, e, nch):
        tp = jnp.minimum(s, n_tiles - 1)
        ip = tp // nf
        return e[ip], jnp.where(nch[ip] > 0, tp % nf, nf - 1)

    def wgu_map(s, e, row, nch, used):
        ex, jj = tile(s, e, nch)
        return (layer, ex, 0, jj)

    def wd_map(s, e, row, nch, used):
        ex, jj = tile(s, e, nch)
        return (layer, ex, jj, 0)

    def bd_map(s, e, row, nch, used):
        return (layer, e[jnp.maximum(s - 1, 0) // nf], 0, 0)

    return pl.pallas_call(
        functools.partial(_expert_body, nf=nf),
        grid_spec=pltpu.PrefetchScalarGridSpec(
            num_scalar_prefetch=4,
            grid=(n_tiles + 1,),
            in_specs=[
                pl.BlockSpec(memory_space=pl.ANY),
                pl.BlockSpec((None, None, d, 2 * tf), wgu_map),
                pl.BlockSpec((None, None, tf, d), wd_map),
                pl.BlockSpec((None, None, 1, 2 * tf), wgu_map),
                pl.BlockSpec((None, None, 1, d), bd_map),
            ],
            out_specs=pl.BlockSpec(memory_space=pl.ANY),
            scratch_shapes=[pltpu.VMEM((2, MOE_PASS, d), BF16), pltpu.VMEM((MOE_PASS, d), F32),
                            pltpu.VMEM((2, MOE_SUB, 2 * tf), F32), pltpu.VMEM((2, d, 2 * tf), BF16),
                            pltpu.VMEM((d // LANE, tf, LANE), F32), pltpu.VMEM((3, tf, d), BF16),
                            pltpu.VMEM((2, 1, 2 * tf), F32),
                            pltpu.VMEM((2, MOE_SUB * ROW_TILES, LANE), F32),
                            pltpu.SemaphoreType.DMA((2,)), pltpu.SemaphoreType.DMA((2,))],
        ),
        out_shape=jax.ShapeDtypeStruct((n_slots * ROW_TILES, LANE), F32),
        compiler_params=_params("arbitrary"),
        name="moe_experts",
    )(p_e, p_row, p_chunks, used_rows, xb, w_gu, w_down, b_gu.reshape(n_l, n_e, 1, 2 * f),
      b_down.reshape(n_l, n_e, 1, d))


def _combine_body(pos_ref, pos_next_ref, x_ref, gate_ref, y_hbm, g_ref, b_ref, o_ref, buf_ref, y_ref, sem,
                  *, alpha):
    tm = x_ref.shape[0]
    i = pl.program_id(0)
    slot = i % 2

    def issue(pos, s):
        for k in range(TOP_K):
            def pair(p, carry, k=k):
                for prio in range(2):
                    t = 2 * p + prio
                    src = pl.ds(pl.multiple_of(pos[k, t] * ROW_TILES, ROW_TILES), ROW_TILES)
                    dst = pl.ds(pl.multiple_of((k * tm + t) * ROW_PITCH, 8), ROW_TILES)
                    pltpu.make_async_copy(y_hbm.at[src], buf_ref.at[s, dst], sem.at[s]).start(priority=prio)
                return carry

            lax.fori_loop(0, tm // 2, pair, 0, unroll=4)

    @pl.when(i == 0)
    def _():
        issue(pos_ref, 0)

    @pl.when(i + 1 < pl.num_programs(0))
    def _():
        issue(pos_next_ref, 1 - slot)

    all_rows = pl.ds(0, TOP_K * tm * ROW_TILES)
    pltpu.make_async_copy(y_hbm.at[all_rows], buf_ref.at[slot, all_rows], sem.at[slot]).wait()
    gate = gate_ref[...]
    parts = [_load_row_major(buf_ref.at[slot], k * tm * ROW_PITCH, tm, ROW_PITCH) for k in range(TOP_K)]
    for c in range(ROW_TILES):
        y_ref[:, c * LANE:(c + 1) * LANE] = (
            (gate[:, 0:1] * parts[0][c] + gate[:, 1:2] * parts[1][c])
            + (gate[:, 2:3] * parts[2][c] + gate[:, 3:4] * parts[3][c]))
    o_ref[...] = _layer_norm_rows(alpha * x_ref[...] + y_ref[...], g_ref[...], b_ref[...])


def _combine_ln(x, yb_rows, pos, gate, g, b, alpha, tm):
    m, d = x.shape
    nt = m // tm
    pos_t = pos.reshape(nt, tm, TOP_K).transpose(0, 2, 1)
    return pl.pallas_call(
        functools.partial(_combine_body, alpha=alpha),
        grid=(nt,),
        in_specs=[pl.BlockSpec((None, TOP_K, tm), lambda i: (i, 0, 0), memory_space=pltpu.SMEM),
                  pl.BlockSpec((None, TOP_K, tm), lambda i: (jnp.minimum(i + 1, nt - 1), 0, 0),
                               memory_space=pltpu.SMEM),
                  pl.BlockSpec((tm, d), lambda i: (i, 0)),
                  pl.BlockSpec((tm, TOP_K), lambda i: (i, 0)),
                  pl.BlockSpec(memory_space=pl.ANY),
                  pl.BlockSpec((1, d), lambda i: (0, 0)), pl.BlockSpec((1, d), lambda i: (0, 0))],
        out_specs=pl.BlockSpec((tm, d), lambda i: (i, 0)),
        out_shape=jax.ShapeDtypeStruct((m, d), F32),
        scratch_shapes=[pltpu.VMEM((2, TOP_K * tm * ROW_PITCH, LANE), F32), pltpu.VMEM((tm, d), F32),
                        pltpu.SemaphoreType.DMA((2,))],
        compiler_params=_params("arbitrary"),
        name="moe_combine_ln",
    )(pos_t, pos_t, x, gate, yb_rows, g.reshape(1, d), b.reshape(1, d))


def _mixer(x, w_in, conv_w, w_gate2, b_gate2, gla_norm_g, w_out, ln_g, ln_b, alpha, bsz, seq):
    d = x.shape[1]
    ch = conv_w.shape[1]
    dk_total = w_gate2.shape[1]
    dk = dk_total // GLA_HEADS
    dv = gla_norm_g.shape[1]
    main_cols = w_in.shape[1] - GLA_GATE_RANK
    w1p = jnp.zeros((d, LANE), BF16).at[:, :GLA_GATE_RANK].set(w_in[:, main_cols:].astype(BF16))
    w2p = jnp.zeros((LANE, dk_total), BF16).at[:GLA_GATE_RANK, :].set(w_gate2.astype(BF16))
    proj, log_a = _in_proj(x, w_in[:, :main_cols].astype(BF16), w1p, w2p, b_gate2.reshape(1, dk_total),
                           tm=1024, tn=512)
    y_conv = _short_conv(proj, conv_w, bsz, seq, ch, tt=512, cw=512)
    q_off = 3 * ch
    k_off = q_off + dk_total
    v_off = k_off + dk_total
    r_off = v_off + GLA_HEADS * dv
    y_gla = _gla(proj, log_a, gla_norm_g, bsz, seq, q_off, k_off, v_off, r_off, dk, dv, tt=512, hpb=2)
    w_out_bf = w_out.astype(BF16)
    return _proj_ln(y_conv, y_gla, w_out_bf[:ch], w_out_bf[ch:], x, ln_g, ln_b, alpha, tm=512)


def _moe(x, x_rows, idx_t, gate_t, layer, w_gu, b_gu, w_down, b_down, ln_g, ln_b, alpha):
    n_e = w_gu.shape[1]
    pos, slot_tok, used_rows, p_e, p_row, p_chunks, n_slots = _dispatch_tables(idx_t, n_e)
    xb = _dispatch_gather(x_rows, slot_tok, used_rows, n_slots)
    yb = _experts(xb, w_gu, w_down, b_gu, b_down, layer, p_e, p_row, p_chunks, used_rows, tf=256)
    return _combine_ln(x, yb, pos, gate_t.T, ln_g, ln_b, alpha, tm=128)


def kernel(x, mem, w_in, conv_w, w_gate2, b_gate2, gla_norm_g, w_out, ln_mix_g, ln_mix_b, w_xq, w_xkv, w_xo, ln_xa_g, ln_xa_b, w_router, b_router, w_gu, b_gu, w_down, b_down, ln_moe_g, ln_moe_b):
    bsz, seq, d = x.shape
    depth = w_in.shape[0]
    alpha = (2 * depth) ** 0.25
    xf = x.reshape(bsz * seq, d)
    memf = mem.reshape(bsz * mem.shape[1], d)
    for l in range(depth):
        xf = _mixer(xf, w_in[l], conv_w[l], w_gate2[l], b_gate2[l], gla_norm_g[l], w_out[l],
                    ln_mix_g[l], ln_mix_b[l], alpha, bsz, seq)
        kv = _matmul(memf, w_xkv[l].astype(BF16), tm=memf.shape[0], tn=512, name="xattn_kv_proj")
        xf, x_rows, idx_t, gate_t = _cross_attn(
            xf, kv, w_xq[l].astype(BF16), w_xo[l].astype(BF16), ln_xa_g[l], ln_xa_b[l],
            w_router[l].T.astype(BF16), b_router[l], alpha, bsz, seq, tm=512)
        xf = _moe(xf, x_rows, idx_t, gate_t, l, w_gu, b_gu, w_down, b_down, ln_moe_g[l], ln_moe_b[l], alpha)
    return xf.reshape(bsz, seq, d)
```

```python
import functools

import numpy as np
import jax
import jax.numpy as jnp
from jax import lax
from jax.experimental import pallas as pl
from jax.experimental.pallas import tpu as pltpu

CONV_W = 3
GLA_HEADS = 4
GLA_GATE_RANK = 16
GLA_TAU = 16.0
XA_HEADS = 4
TOP_K = 4
SWIGLU_LIMIT = 7.0
SWIGLU_ALPHA = 1.702
LN_EPS = 1e-5
NORM_EPS = 1e-6

LANE = 128
GLA_CHUNK = 128
GLA_LEVELS = 7
MOE_SUB = 272
MOE_PASS = 6 * MOE_SUB
ROW_TILES = 16
ROW_PITCH = 24
VMEM_LIMIT = 56 * 1024 * 1024

BF16 = jnp.bfloat16
F32 = jnp.float32


def _dot(a, b):
    return jnp.dot(a, b, preferred_element_type=F32)


def _dot_nt(a, b):
    return lax.dot_general(a, b, (((1,), (1,)), ((), ())), preferred_element_type=F32)


def _dot_tn(a, b):
    return lax.dot_general(a, b, (((0,), (0,)), ((), ())), preferred_element_type=F32)


def _params(*sem):
    return pltpu.CompilerParams(dimension_semantics=sem, vmem_limit_bytes=VMEM_LIMIT)


def _store_row_major(dst_ref, val):
    n = val.shape[0]
    for c in range(ROW_TILES):
        dst_ref[pl.ds(c, n, stride=ROW_TILES), :] = val[:, c * LANE:(c + 1) * LANE]


def _load_row_major(src_ref, first, n, pitch):
    return [src_ref[pl.ds(first + c, n, stride=pitch), :] for c in range(ROW_TILES)]


def _layer_norm_rows(y, g, b):
    mu = jnp.mean(y, axis=-1, keepdims=True)
    yc = y - mu
    var = jnp.mean(yc * yc, axis=-1, keepdims=True)
    return yc * lax.rsqrt(var + LN_EPS) * g + b


def _matmul_body(a_ref, w_ref, o_ref, abf_ref):
    @pl.when(pl.program_id(1) == 0)
    def _():
        abf_ref[...] = a_ref[...].astype(BF16)

    o_ref[...] = _dot(abf_ref[...], w_ref[...]).astype(o_ref.dtype)


def _matmul(a, w, tm, tn, name):
    m, k = a.shape
    n = w.shape[1]
    return pl.pallas_call(
        _matmul_body,
        grid=(m // tm, n // tn),
        in_specs=[pl.BlockSpec((tm, k), lambda i, j: (i, 0)),
                  pl.BlockSpec((k, tn), lambda i, j: (0, j))],
        out_specs=pl.BlockSpec((tm, tn), lambda i, j: (i, j)),
        out_shape=jax.ShapeDtypeStruct((m, n), F32),
        scratch_shapes=[pltpu.VMEM((tm, k), BF16)],
        compiler_params=_params("parallel", "arbitrary"),
        name=name,
    )(a, w)


def _in_proj_body(a_ref, w_ref, w1_ref, w2_ref, b_ref, o_ref, gate_ref, abf_ref):
    @pl.when(pl.program_id(1) == 0)
    def _():
        abf_ref[...] = a_ref[...].astype(BF16)
        low = _dot(abf_ref[...], w1_ref[...])
        z = _dot(low.astype(BF16), w2_ref[...]) + b_ref[...]
        log_sig = jnp.minimum(z, 0.0) - jnp.log1p(jnp.exp(-jnp.abs(z)))
        gate_ref[...] = log_sig / GLA_TAU

    o_ref[...] = _dot(abf_ref[...], w_ref[...])


def _in_proj(a, w, w1p, w2p, b2, tm, tn):
    m, k = a.shape
    n = w.shape[1]
    ng = w2p.shape[1]
    fixed = lambda i, j: (0, 0)
    return pl.pallas_call(
        _in_proj_body,
        grid=(m // tm, n // tn),
        in_specs=[pl.BlockSpec((tm, k), lambda i, j: (i, 0)),
                  pl.BlockSpec((k, tn), lambda i, j: (0, j)),
                  pl.BlockSpec(w1p.shape, fixed), pl.BlockSpec(w2p.shape, fixed),
                  pl.BlockSpec((1, ng), fixed)],
        out_specs=[pl.BlockSpec((tm, tn), lambda i, j: (i, j)),
                   pl.BlockSpec((tm, ng), lambda i, j: (i, 0))],
        out_shape=[jax.ShapeDtypeStruct((m, n), F32), jax.ShapeDtypeStruct((m, ng), F32)],
        scratch_shapes=[pltpu.VMEM((tm, k), BF16)],
        compiler_params=_params("parallel", "arbitrary"),
        name="mixer_in_proj",
    )(a, w, w1p, w2p, b2)


def _conv_body(gc_ref, gb_ref, h_ref, gch_ref, hh_ref, w_ref, o_ref):
    u = gc_ref[...] * h_ref[...]
    uh = jnp.where(pl.program_id(1) == 0, 0.0, gch_ref[...] * hh_ref[...])
    row = lax.broadcasted_iota(jnp.int32, u.shape, 0)
    u1 = jnp.where(row == 0, uh[7:8, :], pltpu.roll(u, 1, 0))
    u2 = pltpu.roll(u, 2, 0)
    u2 = jnp.where(row == 0, uh[6:7, :], jnp.where(row == 1, uh[7:8, :], u2))
    w = w_ref[...]
    conv = w[0:1, :] * u2 + w[1:2, :] * u1 + w[2:3, :] * u
    o_ref[...] = (gb_ref[...] * conv).astype(o_ref.dtype)


def _short_conv(proj, conv_w, bsz, seq, ch, tt, cw):
    nt, nc = seq // tt, ch // cw

    def main(off):
        return pl.BlockSpec((tt, cw), lambda b, i, c: (b * nt + i, off * nc + c))

    def halo(off):
        return pl.BlockSpec(
            (8, cw), lambda b, i, c: (jnp.maximum((b * seq + i * tt) // 8 - 1, 0), off * nc + c))

    return pl.pallas_call(
        _conv_body,
        grid=(bsz, nt, nc),
        in_specs=[main(0), main(1), main(2), halo(0), halo(2),
                  pl.BlockSpec((CONV_W, cw), lambda b, i, c: (0, c))],
        out_specs=pl.BlockSpec((tt, cw), lambda b, i, c: (b * nt + i, c)),
        out_shape=jax.ShapeDtypeStruct((bsz * seq, ch), BF16),
        compiler_params=_params("parallel", "parallel", "parallel"),
        name="short_conv",
    )(proj, proj, proj, proj, proj, conv_w)


def _gla_tables():
    c = GLA_CHUNK
    i = np.arange(c)[:, None]
    t = np.arange(c)[None, :]
    mats = []
    for lvl in range(GLA_LEVELS):
        m = ((i >> (lvl + 1)) << (lvl + 1)) + (1 << lvl) - 1
        upper = (i > m) & (t > m) & (t <= i)
        lower = (i <= m) & (t > i) & (t <= m)
        mats.append(upper | lower)
    mats.append(t <= i)
    return jnp.asarray(np.concatenate(mats, axis=0).astype(np.float32), dtype=BF16)


def _gla_body(q_ref, k_ref, g_ref, v_ref, r_ref, gn_ref, tab_ref, o_ref, st_ref, *, dk_scale):
    c = GLA_CHUNK
    n_heads = st_ref.shape[0]
    dv, dk = st_ref.shape[1:]

    @pl.when(pl.program_id(2) == 0)
    def _():
        st_ref[...] = jnp.zeros_like(st_ref)

    ii = lax.broadcasted_iota(jnp.int32, (c, c), 0)
    jj = lax.broadcasted_iota(jnp.int32, (c, c), 1)
    xor = ii ^ jj
    lower = ii > jj

    for ch in range(q_ref.shape[0] // c):
        rows = pl.ds(ch * c, c)
        for hh in range(n_heads):
            kcols = pl.ds(hh * dk, dk)
            vcols = pl.ds(hh * dv, dv)
            q = q_ref[rows, kcols] * dk_scale
            k = k_ref[rows, kcols]
            g = g_ref[rows, kcols]
            v_bf = v_ref[rows, vcols].astype(BF16)
            g_hi = g.astype(BF16)
            g_lo = (g - g_hi.astype(F32)).astype(BF16)
            sums = _dot(tab_ref[...], jnp.concatenate([g_hi, g_lo], axis=1))
            sums = sums[:, :dk] + sums[:, dk:]

            scores = jnp.where(ii == jj, _dot_nt(q.astype(BF16), k.astype(BF16)), 0.0)
            for lvl in range(GLA_LEVELS):
                w = jnp.exp(sums[lvl * c:(lvl + 1) * c, :])
                s_l = _dot_nt((q * w).astype(BF16), (k * w).astype(BF16))
                scores = jnp.where(((xor >> lvl) == 1) & lower, s_l, scores)

            b = sums[GLA_LEVELS * c:, :]
            b_last = b[c - 1:c, :]
            st = st_ref[hh]
            o = _dot_nt((q * jnp.exp(b)).astype(BF16), st.astype(BF16)) + _dot(scores.astype(BF16), v_bf)
            k_dec = (k * jnp.exp(b_last - b)).astype(BF16)
            st_ref[hh] = st * jnp.exp(b_last) + _dot_tn(v_bf, k_dec)

            o = o * lax.rsqrt(jnp.mean(o * o, axis=-1, keepdims=True) + NORM_EPS) * gn_ref[hh]
            r = r_ref[rows, vcols]
            o_ref[rows, vcols] = (o * (r / (1.0 + jnp.exp(-r)))).astype(o_ref.dtype)


def _gla(proj, log_a, gn, bsz, seq, q_off, k_off, v_off, r_off, dk, dv, tt, hpb):
    nt = seq // tt
    tab = _gla_tables()

    def spec(width, off):
        return pl.BlockSpec((tt, hpb * width), lambda b, h, i: (b * nt + i, off // (hpb * width) + h))

    return pl.pallas_call(
        functools.partial(_gla_body, dk_scale=dk ** -0.5),
        grid=(bsz, GLA_HEADS // hpb, nt),
        in_specs=[spec(dk, q_off), spec(dk, k_off), spec(dk, 0), spec(dv, v_off), spec(dv, r_off),
                  pl.BlockSpec((hpb, 1, dv), lambda b, h, i: (h, 0, 0)),
                  pl.BlockSpec(tab.shape, lambda b, h, i: (0, 0))],
        out_specs=pl.BlockSpec((tt, hpb * dv), lambda b, h, i: (b * nt + i, h)),
        out_shape=jax.ShapeDtypeStruct((bsz * seq, GLA_HEADS * dv), BF16),
        scratch_shapes=[pltpu.VMEM((hpb, dv, dk), F32)],
        compiler_params=_params("parallel", "parallel", "arbitrary"),
        name="gla",
    )(proj, proj, log_a, proj, proj, gn.reshape(GLA_HEADS, 1, dv), tab)


def _proj_ln_body(a0_ref, a1_ref, w0_ref, w1_ref, x_ref, g_ref, b_ref, o_ref, *, alpha):
    y = _dot(a0_ref[...], w0_ref[...]) + _dot(a1_ref[...], w1_ref[...])
    o_ref[...] = _layer_norm_rows(alpha * x_ref[...] + y, g_ref[...], b_ref[...])


def _proj_ln(a0, a1, w0, w1, x, g, b, alpha, tm):
    m, d = x.shape
    row = lambda i: (i, 0)
    fixed = lambda i: (0, 0)
    return pl.pallas_call(
        functools.partial(_proj_ln_body, alpha=alpha),
        grid=(m // tm,),
        in_specs=[pl.BlockSpec((tm, a0.shape[1]), row), pl.BlockSpec((tm, a1.shape[1]), row),
                  pl.BlockSpec(w0.shape, fixed), pl.BlockSpec(w1.shape, fixed),
                  pl.BlockSpec((tm, d), row), pl.BlockSpec((1, d), fixed), pl.BlockSpec((1, d), fixed)],
        out_specs=pl.BlockSpec((tm, d), row),
        out_shape=jax.ShapeDtypeStruct((m, d), F32),
        compiler_params=_params("parallel"),
        name="mixer_out_ln",
    )(a0, a1, w0, w1, x, g.reshape(1, d), b.reshape(1, d))


def _route(x, wt_ref, b_ref, idx_ref, gate_ref):
    logits = _dot_nt(wt_ref[...], x.astype(BF16)) + b_ref[...]
    n_e = logits.shape[0]
    e_iota = lax.broadcasted_iota(jnp.int32, logits.shape, 0)
    vals, idxs = [], []
    for _ in range(TOP_K):
        best = jnp.max(logits, axis=0, keepdims=True)
        sel = jnp.min(jnp.where(logits == best, e_iota, n_e), axis=0, keepdims=True)
        vals.append(best)
        idxs.append(sel)
        logits = jnp.where(e_iota == sel, -jnp.inf, logits)
    exps = [jnp.exp(v - vals[0]) for v in vals]
    total = exps[0] + exps[1] + exps[2] + exps[3]
    idx_ref[...] = jnp.concatenate(idxs, axis=0)
    gate_ref[...] = jnp.concatenate([e / total for e in exps], axis=0)


def _xattn_body(x_ref, kv_ref, wq_ref, wo_ref, g_ref, b_ref, wr_ref, br_ref, o_ref, rows_ref, idx_ref, gate_ref,
                *, alpha):
    x = x_ref[...]
    xa = wq_ref.shape[1]
    hd = xa // XA_HEADS
    q = _dot(x.astype(BF16), wq_ref[...])
    heads = []
    for h in range(XA_HEADS):
        qh = q[:, h * hd:(h + 1) * hd].astype(BF16)
        kh = kv_ref[:, h * hd:(h + 1) * hd].astype(BF16)
        vh = kv_ref[:, xa + h * hd:xa + (h + 1) * hd].astype(BF16)
        s = _dot_nt(qh, kh) * hd ** -0.5
        p = jnp.exp(s - jnp.max(s, axis=-1, keepdims=True))
        p = p / jnp.sum(p, axis=-1, keepdims=True)
        heads.append(_dot(p.astype(BF16), vh))
    o = jnp.concatenate(heads, axis=1).astype(BF16)
    y = _dot(o, wo_ref[...])
    out = _layer_norm_rows(alpha * x + y, g_ref[...], b_ref[...])
    o_ref[...] = out
    _store_row_major(rows_ref, out)
    _route(out, wr_ref, br_ref, idx_ref, gate_ref)


def _cross_attn(x, kv, wq, wo, g, b, wr_t, b_router, alpha, bsz, seq, tm):
    m, d = x.shape
    n_e = wr_t.shape[0]
    nt = seq // tm
    mem_len = kv.shape[0] // bsz
    fixed = lambda bb, i: (0, 0)
    return pl.pallas_call(
        functools.partial(_xattn_body, alpha=alpha),
        grid=(bsz, nt),
        in_specs=[pl.BlockSpec((tm, d), lambda bb, i: (bb * nt + i, 0)),
                  pl.BlockSpec((mem_len, kv.shape[1]), lambda bb, i: (bb, 0)),
                  pl.BlockSpec(wq.shape, fixed), pl.BlockSpec(wo.shape, fixed),
                  pl.BlockSpec((1, d), fixed), pl.BlockSpec((1, d), fixed),
                  pl.BlockSpec((n_e, d), fixed), pl.BlockSpec((n_e, 1), fixed)],
        out_specs=[pl.BlockSpec((tm, d), lambda bb, i: (bb * nt + i, 0)),
                   pl.BlockSpec((tm * ROW_TILES, LANE), lambda bb, i: (bb * nt + i, 0)),
                   pl.BlockSpec((TOP_K, tm), lambda bb, i: (0, bb * nt + i)),
                   pl.BlockSpec((TOP_K, tm), lambda bb, i: (0, bb * nt + i))],
        out_shape=[jax.ShapeDtypeStruct((m, d), F32),
                   jax.ShapeDtypeStruct((m * ROW_TILES, LANE), F32),
                   jax.ShapeDtypeStruct((TOP_K, m), jnp.int32),
                   jax.ShapeDtypeStruct((TOP_K, m), F32)],
        compiler_params=_params("parallel", "parallel"),
        name="cross_attn_ln",
    )(x, kv, wq, wo, g.reshape(1, d), b.reshape(1, d), wr_t, b_router.reshape(n_e, 1))


def _dispatch_tables(idx_t, n_experts):
    n_tok = idx_t.shape[1]
    n_asg = n_tok * TOP_K
    n_slots = (n_asg // MOE_SUB + 1 + n_experts) * MOE_SUB
    n_pass_max = n_asg // MOE_PASS + n_experts
    flat_e = idx_t.T.reshape(n_asg)
    onehot = (flat_e[:, None] == jnp.arange(n_experts, dtype=jnp.int32)[None, :]).astype(jnp.int32)
    csum = jnp.cumsum(onehot, axis=0)
    counts = csum[-1]
    rank = jnp.sum(csum * onehot, axis=1) - 1
    padded = (counts + MOE_SUB - 1) // MOE_SUB * MOE_SUB
    row_end = jnp.cumsum(padded)
    row_start = row_end - padded
    pos = (row_start[flat_e] + rank).astype(jnp.int32)
    asg = jnp.arange(n_asg, dtype=jnp.int32)
    slot_tok = jnp.zeros((n_slots,), jnp.int32).at[pos].set(asg // TOP_K)
    n_pass = (padded + MOE_PASS - 1) // MOE_PASS
    pass_end = jnp.cumsum(n_pass)
    pass_start = pass_end - n_pass
    total = pass_end[-1]
    i = jnp.arange(n_pass_max, dtype=jnp.int32)
    i_eff = jnp.minimum(i, total - 1)
    p_e = jnp.clip(jnp.searchsorted(pass_end, i_eff, side="right"), 0, n_experts - 1).astype(jnp.int32)
    p_off = (i_eff - pass_start[p_e]) * MOE_PASS
    p_row = (row_start[p_e] + p_off).astype(jnp.int32)
    p_chunks = jnp.clip((padded[p_e] - p_off) // MOE_SUB, 0, MOE_PASS // MOE_SUB)
    p_chunks = jnp.where(i < total, p_chunks, 0).astype(jnp.int32)
    used_rows = row_end[-1].astype(jnp.int32).reshape(1)
    return pos, slot_tok, used_rows, p_e, p_row, p_chunks, n_slots


def _gather_body(used_ref, tok_ref, tok_next_ref, x_hbm, o_ref, buf_ref, sem):
    j = pl.program_id(0)
    slot = j % 2

    def issue(toks, s):
        def pair(p, carry):
            for prio in range(2):
                r = 2 * p + prio
                src = pl.ds(pl.multiple_of(toks[0, r] * ROW_TILES, ROW_TILES), ROW_TILES)
                dst = pl.ds(pl.multiple_of(r * ROW_PITCH, 8), ROW_TILES)
                pltpu.make_async_copy(x_hbm.at[src], buf_ref.at[s, dst], sem.at[s]).start(priority=prio)
            return carry

        lax.fori_loop(0, MOE_SUB // 2, pair, 0, unroll=4)

    @pl.when(j == 0)
    def _():
        issue(tok_ref, 0)

    @pl.when((j + 1) * MOE_SUB < used_ref[0])
    def _():
        issue(tok_next_ref, 1 - slot)

    @pl.when(j * MOE_SUB < used_ref[0])
    def _():
        all_rows = pl.ds(0, MOE_SUB * ROW_TILES)
        pltpu.make_async_copy(x_hbm.at[all_rows], buf_ref.at[slot, all_rows], sem.at[slot]).wait()
        for c, part in enumerate(_load_row_major(buf_ref.at[slot], 0, MOE_SUB, ROW_PITCH)):
            o_ref[:, c * LANE:(c + 1) * LANE] = part.astype(o_ref.dtype)

    @pl.when(j * MOE_SUB >= used_ref[0])
    def _():
        o_ref[...] = jnp.zeros_like(o_ref)


def _dispatch_gather(x_rows, slot_tok, used_rows, n_slots):
    d = ROW_TILES * LANE
    n_blk = n_slots // MOE_SUB
    toks = slot_tok.reshape(n_blk, 1, MOE_SUB)
    return pl.pallas_call(
        _gather_body,
        grid_spec=pltpu.PrefetchScalarGridSpec(
            num_scalar_prefetch=1,
            grid=(n_blk,),
            in_specs=[pl.BlockSpec((None, 1, MOE_SUB), lambda j, used: (j, 0, 0),
                                   memory_space=pltpu.SMEM),
                      pl.BlockSpec((None, 1, MOE_SUB), lambda j, used: (jnp.minimum(j + 1, n_blk - 1), 0, 0),
                                   memory_space=pltpu.SMEM),
                      pl.BlockSpec(memory_space=pl.ANY)],
            out_specs=pl.BlockSpec((MOE_SUB, d), lambda j, used: (j, 0)),
            scratch_shapes=[pltpu.VMEM((2, MOE_SUB * ROW_PITCH, LANE), F32), pltpu.SemaphoreType.DMA((2,))],
        ),
        out_shape=jax.ShapeDtypeStruct((n_slots, d), BF16),
        compiler_params=_params("arbitrary"),
        name="moe_gather",
    )(used_rows, toks, toks, x_rows)


def _expert_body(e_ref, row_ref, nch_ref, used_ref, x_hbm, wgu_ref, wd_ref, bgu_ref, bd_ref, y_hbm,
                 x_buf, acc, gu_buf, wgu_bf, wd_perm, wd_bf, y_stage, x_sem, y_sem):
    i, j = pl.program_id(0), pl.program_id(1)
    n_items, last = pl.num_programs(0), pl.num_programs(1) - 1
    nch = nch_ref[i]
    slot = i % 2
    tf = wd_ref.shape[0]
    half = tf // 2

    def chunk(c):
        return pl.ds(pl.multiple_of(c * MOE_SUB, MOE_SUB), MOE_SUB)

    def x_copy(item, c):
        rows = pl.ds(pl.multiple_of(row_ref[item] + c * MOE_SUB, MOE_SUB), MOE_SUB)
        return pltpu.make_async_copy(x_hbm.at[rows], x_buf.at[item % 2, chunk(c)], x_sem.at[item % 2])

    def y_copy(first_row, s):
        rows = pl.ds(pl.multiple_of(first_row * ROW_TILES, MOE_SUB * ROW_TILES), MOE_SUB * ROW_TILES)
        return pltpu.make_async_copy(y_stage.at[s], y_hbm.at[rows], y_sem.at[s])

    def y_drain(n_chunks):
        for s in range(2):
            @pl.when(n_chunks > s)
            def _():
                y_copy(0, s).wait()

    def for_chunks(item, fn):
        def step(c, carry):
            fn(item, c)
            return carry

        lax.fori_loop(0, nch_ref[item], step, 0)

    @pl.when(j == 0)
    def _():
        @pl.when(i == 0)
        def _():
            for_chunks(0, lambda it, c: x_copy(it, c).start())

        @pl.when(i + 1 < n_items)
        def _():
            for_chunks(i + 1, lambda it, c: x_copy(it, c).start())

        for_chunks(i, lambda it, c: x_copy(it, c).wait())

        @pl.when(nch > 0)
        def _():
            acc[chunk(nch - 1), :] = jnp.broadcast_to(bd_ref[...], (MOE_SUB, acc.shape[1]))

    even_lane = lax.broadcasted_iota(jnp.int32, (1, tf), 1) % 2 == 0

    def up_proj(c):
        return _dot(x_buf[slot, chunk(c), :], wgu_bf[...]) + bgu_ref[...]

    def down_proj(gu, wd):
        up = jnp.clip(pltpu.roll(gu, 2 * tf - 1, 1), -SWIGLU_LIMIT, SWIGLU_LIMIT)
        gate = jnp.minimum(gu, SWIGLU_LIMIT)
        act = (up + 1.0) * (gate / (1.0 + jnp.exp(-SWIGLU_ALPHA * gate)))
        packed = jnp.where(even_lane, act[:, :tf], pltpu.roll(act[:, tf:], 1, 1))
        return _dot(packed.astype(BF16), wd)

    wd_cur = j % 2

    def prepare_weights():
        wgu_bf[...] = wgu_ref[...].astype(BF16)
        for cb in range(wd_ref.shape[1] // LANE):
            cols = pl.ds(cb * LANE, LANE)
            wd_perm[cb, pl.ds(0, half, stride=2), :] = wd_ref[0:half, cols]
            wd_perm[cb, pl.ds(1, half, stride=2), :] = wd_ref[half:tf, cols]
            wd_bf[wd_cur, :, cols] = wd_perm[cb].astype(BF16)

    def carried_trip(k, carry):
        first = k == 0
        c_fin = jnp.where(first, nch - 1, k - 1)
        gu = gu_buf[c_fin % 2]
        gu_next = up_proj(k)
        acc[chunk(c_fin), :] += down_proj(gu, wd_bf[jnp.where(first, 1 - wd_cur, wd_cur)])
        gu_buf[k % 2] = gu_next
        return carry

    def plain_trip(k, carry, emit, first_tile=False):
        c_fin = k - 1
        if emit:
            @pl.when(c_fin >= 2)
            def _():
                y_copy(0, c_fin % 2).wait()
        gu = gu_buf[c_fin % 2]
        gu_next = up_proj(k)
        y = down_proj(gu, wd_bf[wd_cur])
        if emit:
            _store_row_major(y_stage.at[c_fin % 2], acc[chunk(c_fin), :] + y)
            y_copy(row_ref[i] + c_fin * MOE_SUB, c_fin % 2).start()
        elif first_tile:
            acc[chunk(c_fin), :] = bd_ref[...] + y
        else:
            acc[chunk(c_fin), :] += y
        gu_buf[k % 2] = gu_next
        return carry

    @pl.when((nch > 0) & (j == 0))
    def _():
        prepare_weights()
        gu_buf[0] = up_proj(0)
        lax.fori_loop(1, nch, functools.partial(plain_trip, emit=False, first_tile=True), 0)

    @pl.when((nch > 0) & (j > 0) & (j < last))
    def _():
        prepare_weights()
        lax.fori_loop(0, nch, carried_trip, 0)

    @pl.when(j == last)
    def _():
        @pl.when(i > 0)
        def _():
            y_drain(nch_ref[i - 1])

        @pl.when(nch > 0)
        def _():
            prepare_weights()
            carried_trip(jnp.int32(0), 0)
            lax.fori_loop(1, nch, functools.partial(plain_trip, emit=True), 0)
            c_fin = nch - 1

            @pl.when(c_fin >= 2)
            def _():
                y_copy(0, c_fin % 2).wait()

            y = down_proj(gu_buf[c_fin % 2], wd_bf[wd_cur])
            _store_row_major(y_stage.at[c_fin % 2], acc[chunk(c_fin), :] + y)
            y_copy(row_ref[i] + c_fin * MOE_SUB, c_fin % 2).start()

        @pl.when(i == n_items - 1)
        def _():
            y_drain(nch)
            y_stage[0] = jnp.zeros(y_stage.shape[1:], y_stage.dtype)

            def fill_start(c, carry):
                y_copy(c * MOE_SUB, 0).start()
                return carry

            def fill_wait(c, carry):
                y_copy(0, 0).wait()
                return carry

            first, end = used_ref[0] // MOE_SUB, y_hbm.shape[0] // (MOE_SUB * ROW_TILES)
            lax.fori_loop(first, end, fill_start, 0)
            lax.fori_loop(first, end, fill_wait, 0)


def _experts(xb, w_gu, w_down, b_gu, b_down, layer, p_e, p_row, p_chunks, used_rows, tf):
    n_slots, d = xb.shape
    n_l, n_e, f, _ = w_down.shape
    nf = f // tf
    assert nf >= 2
    n_items = p_e.shape[0]

    def jeff(i, j, nch):
        return jnp.where(nch[i] > 0, j, nf - 1)

    return pl.pallas_call(
        _expert_body,
        grid_spec=pltpu.PrefetchScalarGridSpec(
            num_scalar_prefetch=4,
            grid=(n_items, nf),
            in_specs=[
                pl.BlockSpec(memory_space=pl.ANY),
                pl.BlockSpec((None, None, d, 2 * tf), lambda i, j, e, row, nch, used: (layer, e[i], 0, jeff(i, j, nch))),
                pl.BlockSpec((None, None, tf, d), lambda i, j, e, row, nch, used: (layer, e[i], jeff(i, j, nch), 0)),
                pl.BlockSpec((None, None, 1, 2 * tf), lambda i, j, e, row, nch, used: (layer, e[i], 0, jeff(i, j, nch))),
                pl.BlockSpec((None, None, 1, d), lambda i, j, e, row, nch, used: (layer, e[i], 0, 0)),
            ],
            out_specs=pl.BlockSpec(memory_space=pl.ANY),
            scratch_shapes=[pltpu.VMEM((2, MOE_PASS, d), BF16), pltpu.VMEM((MOE_PASS, d), F32),
                            pltpu.VMEM((2, MOE_SUB, 2 * tf), F32), pltpu.VMEM((d, 2 * tf), BF16),
                            pltpu.VMEM((d // LANE, tf, LANE), F32), pltpu.VMEM((2, tf, d), BF16),
                            pltpu.VMEM((2, MOE_SUB * ROW_TILES, LANE), F32),
                            pltpu.SemaphoreType.DMA((2,)), pltpu.SemaphoreType.DMA((2,))],
        ),
        out_shape=jax.ShapeDtypeStruct((n_slots * ROW_TILES, LANE), F32),
        compiler_params=_params("arbitrary", "arbitrary"),
        name="moe_experts",
    )(p_e, p_row, p_chunks, used_rows, xb, w_gu, w_down, b_gu.reshape(n_l, n_e, 1, 2 * f),
      b_down.reshape(n_l, n_e, 1, d))


def _combine_body(pos_ref, pos_next_ref, x_ref, gate_ref, y_hbm, g_ref, b_ref, o_ref, buf_ref, y_ref, sem,
                  *, alpha):
    tm = x_ref.shape[0]
    i = pl.program_id(0)
    slot = i % 2

    def issue(pos, s):
        for k in range(TOP_K):
            def pair(p, carry, k=k):
                for prio in range(2):
                    t = 2 * p + prio
                    src = pl.ds(pl.multiple_of(pos[k, t] * ROW_TILES, ROW_TILES), ROW_TILES)
                    dst = pl.ds(pl.multiple_of((k * tm + t) * ROW_PITCH, 8), ROW_TILES)
                    pltpu.make_async_copy(y_hbm.at[src], buf_ref.at[s, dst], sem.at[s]).start(priority=prio)
                return carry

            lax.fori_loop(0, tm // 2, pair, 0, unroll=4)

    @pl.when(i == 0)
    def _():
        issue(pos_ref, 0)

    @pl.when(i + 1 < pl.num_programs(0))
    def _():
        issue(pos_next_ref, 1 - slot)

    all_rows = pl.ds(0, TOP_K * tm * ROW_TILES)
    pltpu.make_async_copy(y_hbm.at[all_rows], buf_ref.at[slot, all_rows], sem.at[slot]).wait()
    gate = gate_ref[...]
    parts = [_load_row_major(buf_ref.at[slot], k * tm * ROW_PITCH, tm, ROW_PITCH) for k in range(TOP_K)]
    for c in range(ROW_TILES):
        y_ref[:, c * LANE:(c + 1) * LANE] = (
            (gate[:, 0:1] * parts[0][c] + gate[:, 1:2] * parts[1][c])
            + (gate[:, 2:3] * parts[2][c] + gate[:, 3:4] * parts[3][c]))
    o_ref[...] = _layer_norm_rows(alpha * x_ref[...] + y_ref[...], g_ref[...], b_ref[...])


def _combine_ln(x, yb_rows, pos, gate, g, b, alpha, tm):
    m, d = x.shape
    nt = m // tm
    pos_t = pos.reshape(nt, tm, TOP_K).transpose(0, 2, 1)
    return pl.pallas_call(
        functools.partial(_combine_body, alpha=alpha),
        grid=(nt,),
        in_specs=[pl.BlockSpec((None, TOP_K, tm), lambda i: (i, 0, 0), memory_space=pltpu.SMEM),
                  pl.BlockSpec((None, TOP_K, tm), lambda i: (jnp.minimum(i + 1, nt - 1), 0, 0),
                               memory_space=pltpu.SMEM),
                  pl.BlockSpec((tm, d), lambda i: (i, 0)),
                  pl.BlockSpec((tm, TOP_K), lambda i: (i, 0)),
                  pl.BlockSpec(memory_space=pl.ANY),
                  pl.BlockSpec((1, d), lambda i: (0, 0)), pl.BlockSpec((1, d), lambda i: (0, 0))],
        out_specs=pl.BlockSpec((tm, d), lambda i: (i, 0)),
        out_shape=jax.ShapeDtypeStruct((m, d), F32),
        scratch_shapes=[pltpu.VMEM((2, TOP_K * tm * ROW_PITCH, LANE), F32), pltpu.VMEM((tm, d), F32),
                        pltpu.SemaphoreType.DMA((2,))],
        compiler_params=_params("arbitrary"),
        name="moe_combine_ln",
    )(pos_t, pos_t, x, gate, yb_rows, g.reshape(1, d), b.reshape(1, d))


def _mixer(x, w_in, conv_w, w_gate2, b_gate2, gla_norm_g, w_out, ln_g, ln_b, alpha, bsz, seq):
    d = x.shape[1]
    ch = conv_w.shape[1]
    dk_total = w_gate2.shape[1]
    dk = dk_total // GLA_HEADS
    dv = gla_norm_g.shape[1]
    main_cols = w_in.shape[1] - GLA_GATE_RANK
    w1p = jnp.zeros((d, LANE), BF16).at[:, :GLA_GATE_RANK].set(w_in[:, main_cols:].astype(BF16))
    w2p = jnp.zeros((LANE, dk_total), BF16).at[:GLA_GATE_RANK, :].set(w_gate2.astype(BF16))
    proj, log_a = _in_proj(x, w_in[:, :main_cols].astype(BF16), w1p, w2p, b_gate2.reshape(1, dk_total),
                           tm=1024, tn=512)
    y_conv = _short_conv(proj, conv_w, bsz, seq, ch, tt=512, cw=512)
    q_off = 3 * ch
    k_off = q_off + dk_total
    v_off = k_off + dk_total
    r_off = v_off + GLA_HEADS * dv
    y_gla = _gla(proj, log_a, gla_norm_g, bsz, seq, q_off, k_off, v_off, r_off, dk, dv, tt=512, hpb=2)
    w_out_bf = w_out.astype(BF16)
    return _proj_ln(y_conv, y_gla, w_out_bf[:ch], w_out_bf[ch:], x, ln_g, ln_b, alpha, tm=512)


def _moe(x, x_rows, idx_t, gate_t, layer, w_gu, b_gu, w_down, b_down, ln_g, ln_b, alpha):
    n_e = w_gu.shape[1]
    pos, slot_tok, used_rows, p_e, p_row, p_chunks, n_slots = _dispatch_tables(idx_t, n_e)
    xb = _dispatch_gather(x_rows, slot_tok, used_rows, n_slots)
    yb = _experts(xb, w_gu, w_down, b_gu, b_down, layer, p_e, p_row, p_chunks, used_rows, tf=256)
    return _combine_ln(x, yb, pos, gate_t.T, ln_g, ln_b, alpha, tm=128)


def kernel(x, mem, w_in, conv_w, w_gate2, b_gate2, gla_norm_g, w_out, ln_mix_g, ln_mix_b, w_xq, w_xkv, w_xo, ln_xa_g, ln_xa_b, w_router, b_router, w_gu, b_gu, w_down, b_down, ln_moe_g, ln_moe_b):
    bsz, seq, d = x.shape
    depth = w_in.shape[0]
    alpha = (2 * depth) ** 0.25
    xf = x.reshape(bsz * seq, d)
    memf = mem.reshape(bsz * mem.shape[1], d)
    for l in range(depth):
        xf = _mixer(xf, w_in[l], conv_w[l], w_gate2[l], b_gate2[l], gla_norm_g[l], w_out[l],
                    ln_mix_g[l], ln_mix_b[l], alpha, bsz, seq)
        kv = _matmul(memf, w_xkv[l].astype(BF16), tm=memf.shape[0], tn=512, name="xattn_kv_proj")
        xf, x_rows, idx_t, gate_t = _cross_attn(
            xf, kv, w_xq[l].astype(BF16), w_xo[l].astype(BF16), ln_xa_g[l], ln_xa_b[l],
            w_router[l].T.astype(BF16), b_router[l], alpha, bsz, seq, tm=512)
        xf = _moe(xf, x_rows, idx_t, gate_t, l, w_gu, b_gu, w_down, b_down, ln_moe_g[l], ln_moe_b[l], alpha)
    return xf.reshape(bsz, seq, d)
```
